```python
import math
import jax, jax.numpy as jnp
from jax import lax
import numpy as np

D_MODEL = 1024
BATCH = 4
SEQ = 4096
DEPTH = 4
DEC_BATCH = 32
DEC_SEQ = 8
PAST_LEN = 8192
PAGE_SIZE = 128

N_BRANCH = 4
BR_WIDTH = 1024
CONV_A_WIDTH = 31
HEADS_B = 8
KV_HEADS_B = 4
GROUP_B = HEADS_B // KV_HEADS_B
HEAD_DIM_B = 64
QBLK = 128
D_INNER_C = 1024
HEAD_DIM_C = 64
HEADS_C = D_INNER_C // HEAD_DIM_C
GROUPS_C = 4
D_STATE_C = 128
CONV_C_WIDTH = 4
CONV_C_DIM = D_INNER_C + 2 * GROUPS_C * D_STATE_C
SSD_CHUNK = 128
N_MEM = 256
HEADS_M = 4
HEAD_DIM_M = BR_WIDTH // HEADS_M

ALPHA = (2 * DEPTH) ** 0.25
BETA = (8 * DEPTH) ** -0.25
EPS = 1e-5

SPLIT_SIZES = (
    2 * BR_WIDTH, BR_WIDTH,
    HEADS_B * 2 * HEAD_DIM_B, KV_HEADS_B * 2 * HEAD_DIM_B,
    KV_HEADS_B * 2 * HEAD_DIM_B, BR_WIDTH,
    D_INNER_C, CONV_C_DIM, HEADS_C,
    BR_WIDTH, BR_WIDTH,
    N_BRANCH * D_MODEL)
IN_COLS = sum(SPLIT_SIZES)

kernel_name = 'hybrid_conformer_diffattn_ssd_mem_step'


def _layernorm(x, g, b):
    xf = x.astype(jnp.float32)
    mu = jnp.mean(xf, -1, keepdims=True)
    var = jnp.mean(jnp.square(xf - mu), -1, keepdims=True)
    return ((xf - mu) * lax.rsqrt(var + EPS)).astype(x.dtype) * g + b


def _rmsnorm(x, w):
    xf = x.astype(jnp.float32)
    return (xf * lax.rsqrt(jnp.mean(xf * xf, -1, keepdims=True) + EPS)).astype(x.dtype) * w


def _split_cols(cols):
    out, start = [], 0
    for n in SPLIT_SIZES:
        out.append(cols[..., start:start + n])
        start += n
    return out


def _causal_dwconv(x, buf, w, b):
    xp = jnp.concatenate([buf.astype(x.dtype), x], axis=1)
    y = lax.conv_general_dilated(xp, w[:, None, :].astype(x.dtype), window_strides=(1,), padding='VALID',
                                 dimension_numbers=('NWC', 'WIO', 'NWC'), feature_group_count=x.shape[-1])
    return y + b, xp[:, xp.shape[1] - (w.shape[0] - 1):]


def _diff_weights(scores, lam):
    p = jax.nn.softmax(scores.astype(jnp.float32), axis=-1)
    return p[..., 0, :, :] - lam * p[..., 1, :, :]


def _diff_attn_prompt(q, k, v, lam):
    B, S = q.shape[:2]
    blk = min(QBLK, S)
    nb = S // blk
    qb = jnp.moveaxis(q.reshape(B, nb, blk, *q.shape[2:]), 1, 0)
    kpos = jnp.arange(S)
    scale = HEAD_DIM_B ** -0.5

    def one_block(args):
        qi, start = args
        s = jnp.einsum('bthgcd,bshcd->bhgcts', qi, k) * scale
        qpos = start + jnp.arange(blk)
        s = jnp.where(qpos[:, None] >= kpos[None, :], s, -jnp.inf)
        a = _diff_weights(s, lam).astype(v.dtype)
        return jnp.einsum('bhgts,bshe->bthge', a, v)

    out = lax.map(one_block, (qb, jnp.arange(nb) * blk))
    return jnp.moveaxis(out, 0, 1).reshape(B, S, HEADS_B, 2 * HEAD_DIM_B)


def _diff_attn_sample(q, k, v, k_past, v_past, lam):
    B, T = q.shape[:2]
    P = k_past.shape[1]
    scale = HEAD_DIM_B ** -0.5
    s_p = jnp.einsum('bthgcd,bphcd->bhgctp', q, k_past) * scale
    s_n = jnp.einsum('bthgcd,bshcd->bhgcts', q, k) * scale
    s_n = jnp.where(jnp.tril(jnp.ones((T, T), bool)), s_n, -jnp.inf)
    a = _diff_weights(jnp.concatenate([s_p, s_n], axis=-1), lam).astype(v.dtype)
    out = (jnp.einsum('bhgtp,bphe->bthge', a[..., :P], v_past)
           + jnp.einsum('bhgts,bshe->bthge', a[..., P:], v))
    return out.reshape(B, T, HEADS_B, 2 * HEAD_DIM_B)


def _ssd(x, dt, A, Bm, Cm, h0, chunk):
    Bsz, L, H, P = x.shape
    G, N = Bm.shape[2], Bm.shape[3]
    Hg = H // G
    nc = L // chunk
    f32 = jnp.float32

    def to_chunks(a):
        return jnp.moveaxis(a.reshape(Bsz, nc, chunk, *a.shape[2:]), 1, 0)

    xc = to_chunks(x.astype(f32).reshape(Bsz, L, G, Hg, P))
    dtc = to_chunks(dt.reshape(Bsz, L, G, Hg))
    bc = to_chunks(Bm.astype(f32))
    cc = to_chunks(Cm.astype(f32))
    Ag = A.reshape(G, Hg)
    causal = jnp.tril(jnp.ones((chunk, chunk), bool))

    def step(h, inp):
        xk, dtk, bk, ck = inp
        cum = jnp.cumsum(dtk * Ag, axis=1)
        seg = jnp.where(causal[None, :, :, None, None], cum[:, :, None] - cum[:, None, :], -jnp.inf)
        decay = jnp.exp(seg)
        xdt = xk * dtk[..., None]
        cb = jnp.einsum('btgn,bsgn->btsg', ck, bk)
        y = jnp.einsum('btsg,btsgh,bsghp->btghp', cb, decay, xdt)
        y = y + jnp.einsum('btgn,bghpn->btghp', ck, h) * jnp.exp(cum)[..., None]
        to_end = jnp.exp(cum[:, -1:] - cum)
        h = (h * jnp.exp(cum[:, -1])[..., None, None]
             + jnp.einsum('bsgn,bsghp->bghpn', bk, xdt * to_end[..., None]))
        return h, y

    h, ys = lax.scan(step, h0.astype(f32).reshape(Bsz, G, Hg, P, N), (xc, dtc, bc, cc))
    y = jnp.moveaxis(ys, 0, 1).reshape(Bsz, L, H, P).astype(x.dtype)
    return y, h.reshape(Bsz, H, P, N)


def _mamba(z, xbc, dt_raw, conv_buf, h0, lp):
    B, L, _ = z.shape
    xbc, new_conv = _causal_dwconv(xbc, conv_buf, lp['conv_c_w'], lp['conv_c_b'])
    xbc = jax.nn.silu(xbc)
    nb = GROUPS_C * D_STATE_C
    xs = xbc[..., :D_INNER_C].reshape(B, L, HEADS_C, HEAD_DIM_C)
    Bm = xbc[..., D_INNER_C:D_INNER_C + nb].reshape(B, L, GROUPS_C, D_STATE_C)
    Cm = xbc[..., D_INNER_C + nb:].reshape(B, L, GROUPS_C, D_STATE_C)
    dt = jax.nn.softplus((dt_raw + lp['dt_bias']).astype(jnp.float32))
    A = -jnp.exp(lp['a_log'].astype(jnp.float32))
    chunk = SSD_CHUNK if L % SSD_CHUNK == 0 else L
    y, h = _ssd(xs, dt, A, Bm, Cm, h0, chunk)
    y = (y + lp['d_skip'][:, None] * xs).reshape(B, L, D_INNER_C)
    y = _rmsnorm(y * jax.nn.silu(z), lp['norm_c_w'])
    return y, new_conv, h.astype(z.dtype)


def _mem_attn(q, mk, mv):
    B, T = q.shape[:2]
    q = q.reshape(B, T, HEADS_M, HEAD_DIM_M)
    s = jnp.einsum('bthd,bmhd->bhtm', q, mk) * HEAD_DIM_M ** -0.5
    p = jax.nn.softmax(s.astype(jnp.float32), axis=-1).astype(mv.dtype)
    return jnp.einsum('bhtm,bmhd->bthd', p, mv).reshape(B, T, BR_WIDTH)


def _layer(x, lp, l, attend, mem_k, mem_v, conv_a_buf, conv_c_buf, ssm_h):
    B, T, _ = x.shape
    (a_glu, a_gate, b_q, b_k, b_v, b_gate, c_z, c_xbc, c_dt, m_q, m_gate, g) = _split_cols(x @ lp['w_in'])
    a1, a2 = jnp.split(a_glu, 2, axis=-1)
    h_a, new_conv_a = _causal_dwconv(a1 * jax.nn.sigmoid(a2), conv_a_buf, lp['conv_a_w'], lp['conv_a_b'])
    y_a = jax.nn.silu(_layernorm(h_a, lp['ln_a_g'], lp['ln_a_b'])) * jax.nn.silu(a_gate)
    q = b_q.reshape(B, T, KV_HEADS_B, GROUP_B, 2, HEAD_DIM_B)
    k_row = b_k.reshape(B, T, KV_HEADS_B, 2 * HEAD_DIM_B)
    v_row = b_v.reshape(B, T, KV_HEADS_B, 2 * HEAD_DIM_B)
    lam_init = 0.8 - 0.6 * math.exp(-0.3 * l)
    f32 = jnp.float32
    lam = (jnp.exp(jnp.sum(lp['lam_q1'].astype(f32) * lp['lam_k1'].astype(f32)))
           - jnp.exp(jnp.sum(lp['lam_q2'].astype(f32) * lp['lam_k2'].astype(f32))) + lam_init)
    o = attend(q, k_row.reshape(B, T, KV_HEADS_B, 2, HEAD_DIM_B), v_row, lam)
    y_b = (_rmsnorm(o, lp['subln_w']) * (1.0 - lam_init)).reshape(B, T, BR_WIDTH) * jax.nn.silu(b_gate)
    y_c, new_conv_c, new_h = _mamba(c_z, c_xbc, c_dt, conv_c_buf, ssm_h, lp)
    y_m = _mem_attn(m_q, mem_k, mem_v) * jax.nn.silu(m_gate)
    br = jnp.stack([y_a, y_b, y_c, y_m], axis=2)
    proj = jnp.einsum('btkc,kcd->btkd', br, lp['w_branch'])
    gates = jax.nn.sigmoid(g + lp['b_gate']).reshape(B, T, N_BRANCH, D_MODEL)
    out = jnp.sum(gates * proj, axis=2) @ lp['w_out']
    x_new = _layernorm(ALPHA * x + out, lp['ln_g'], lp['ln_b'])
    return x_new, k_row, v_row, new_conv_a, new_conv_c, new_h


def setup_inputs(seed: int = 0) -> dict:
    key = jax.random.key(seed)
    keys = iter(jax.random.split(key, 48))
    f32 = jnp.float32

    def nrm(shape, scale=1.0):
        return jax.random.normal(next(keys), shape, f32) * scale

    def gain(shape):
        return 1.0 + nrm(shape, 0.02)

    n_pages = PAST_LEN // PAGE_SIZE
    n_phys = (DEC_BATCH * n_pages * 5 + 3) // 4
    dt0 = jnp.exp(jax.random.uniform(next(keys), (DEPTH, HEADS_C), f32, math.log(1e-3), math.log(1e-1)))
    dt_bias = dt0 + jnp.log(-jnp.expm1(-dt0))
    a_log = jnp.log(jax.random.uniform(next(keys), (DEPTH, HEADS_C), f32, 1.0, 16.0))
    page_table = jax.random.permutation(next(keys), n_phys)[:DEC_BATCH * n_pages].reshape(
        DEC_BATCH, n_pages).astype(jnp.int32)
    return dict(
        x_prompt=nrm((BATCH, SEQ, D_MODEL)),
        x_sample=nrm((DEC_BATCH, DEC_SEQ, D_MODEL)),
        mem_prompt=nrm((BATCH, N_MEM, D_MODEL)),
        cache_k=nrm((n_phys, DEPTH, PAGE_SIZE, KV_HEADS_B, 2 * HEAD_DIM_B)),
        cache_v=nrm((n_phys, DEPTH, PAGE_SIZE, KV_HEADS_B, 2 * HEAD_DIM_B)),
        page_table=page_table,
        cache_mem_k=nrm((DEC_BATCH, DEPTH, N_MEM, HEADS_M, HEAD_DIM_M)),
        cache_mem_v=nrm((DEC_BATCH, DEPTH, N_MEM, HEADS_M, HEAD_DIM_M)),
        state_conv_a=nrm((DEC_BATCH, DEPTH, CONV_A_WIDTH - 1, BR_WIDTH), 0.5),
        state_conv_c=nrm((DEC_BATCH, DEPTH, CONV_C_WIDTH - 1, CONV_C_DIM)),
        state_ssm=nrm((DEC_BATCH, DEPTH, HEADS_C, HEAD_DIM_C, D_STATE_C), 0.3),
        w_in=nrm((DEPTH, D_MODEL, IN_COLS), D_MODEL ** -0.5),
        b_gate=nrm((DEPTH, N_BRANCH * D_MODEL), 0.01),
        conv_a_w=nrm((DEPTH, CONV_A_WIDTH, BR_WIDTH), CONV_A_WIDTH ** -0.5),
        conv_a_b=nrm((DEPTH, BR_WIDTH), 0.01),
        ln_a_g=gain((DEPTH, BR_WIDTH)),
        ln_a_b=nrm((DEPTH, BR_WIDTH), 0.01),
        lam_q1=nrm((DEPTH, HEAD_DIM_B), 0.1),
        lam_k1=nrm((DEPTH, HEAD_DIM_B), 0.1),
        lam_q2=nrm((DEPTH, HEAD_DIM_B), 0.1),
        lam_k2=nrm((DEPTH, HEAD_DIM_B), 0.1),
        subln_w=gain((DEPTH, 2 * HEAD_DIM_B)),
        conv_c_w=nrm((DEPTH, CONV_C_WIDTH, CONV_C_DIM), CONV_C_WIDTH ** -0.5),
        conv_c_b=nrm((DEPTH, CONV_C_DIM), 0.01),
        dt_bias=dt_bias,
        a_log=a_log,
        d_skip=gain((DEPTH, HEADS_C)),
        norm_c_w=gain((DEPTH, D_INNER_C)),
        w_mem_kv=nrm((DEPTH, D_MODEL, 2 * BR_WIDTH), D_MODEL ** -0.5),
        w_branch=nrm((DEPTH, N_BRANCH, BR_WIDTH, D_MODEL), BETA * BR_WIDTH ** -0.5),
        w_out=nrm((DEPTH, D_MODEL, D_MODEL), BETA * D_MODEL ** -0.5),
        ln_g=gain((DEPTH, D_MODEL)),
        ln_b=nrm((DEPTH, D_MODEL), 0.01),
    )


def reference(x_prompt, x_sample, mem_prompt, cache_k, cache_v, page_table, cache_mem_k, cache_mem_v,
              state_conv_a, state_conv_c, state_ssm, w_in, b_gate, conv_a_w, conv_a_b, ln_a_g, ln_a_b,
              lam_q1, lam_k1, lam_q2, lam_k2, subln_w, conv_c_w, conv_c_b, dt_bias, a_log, d_skip,
              norm_c_w, w_mem_kv, w_branch, w_out, ln_g, ln_b):
    B, S = x_prompt.shape[:2]
    DB = x_sample.shape[0]
    ps = cache_k.shape[2]
    past = page_table.shape[1] * ps
    xp, xs = x_prompt, x_sample
    kp_l, vp_l, mkp_l, mvp_l, cap_l, ccp_l, hp_l = [], [], [], [], [], [], []
    ks_l, vs_l, cas_l, ccs_l, hs_l = [], [], [], [], []
    for l in range(DEPTH):
        lp = dict(w_in=w_in[l], b_gate=b_gate[l], conv_a_w=conv_a_w[l], conv_a_b=conv_a_b[l],
                  ln_a_g=ln_a_g[l], ln_a_b=ln_a_b[l], lam_q1=lam_q1[l], lam_k1=lam_k1[l],
                  lam_q2=lam_q2[l], lam_k2=lam_k2[l], subln_w=subln_w[l], conv_c_w=conv_c_w[l],
                  conv_c_b=conv_c_b[l], dt_bias=dt_bias[l], a_log=a_log[l], d_skip=d_skip[l],
                  norm_c_w=norm_c_w[l], w_branch=w_branch[l], w_out=w_out[l], ln_g=ln_g[l], ln_b=ln_b[l])
        mkv = mem_prompt @ w_mem_kv[l]
        mk = mkv[..., :BR_WIDTH].reshape(B, N_MEM, HEADS_M, HEAD_DIM_M)
        mv = mkv[..., BR_WIDTH:].reshape(B, N_MEM, HEADS_M, HEAD_DIM_M)
        xp, kr, vr, ca, cc, hh = _layer(
            xp, lp, l, _diff_attn_prompt, mk, mv,
            jnp.zeros((B, CONV_A_WIDTH - 1, BR_WIDTH), xp.dtype),
            jnp.zeros((B, CONV_C_WIDTH - 1, CONV_C_DIM), xp.dtype),
            jnp.zeros((B, HEADS_C, HEAD_DIM_C, D_STATE_C), jnp.float32))
        kp_l.append(kr); vp_l.append(vr); mkp_l.append(mk); mvp_l.append(mv)
        cap_l.append(ca); ccp_l.append(cc); hp_l.append(hh)
        k_past = cache_k[page_table, l].reshape(DB, past, KV_HEADS_B, 2, HEAD_DIM_B)
        v_past = cache_v[page_table, l].reshape(DB, past, KV_HEADS_B, 2 * HEAD_DIM_B)

        def attend(q, k, v, lam, k_past=k_past, v_past=v_past):
            return _diff_attn_sample(q, k, v, k_past, v_past, lam)

        xs, kr, vr, ca, cc, hh = _layer(
            xs, lp, l, attend, cache_mem_k[:, l], cache_mem_v[:, l],
            state_conv_a[:, l], state_conv_c[:, l], state_ssm[:, l])
        ks_l.append(kr); vs_l.append(vr); cas_l.append(ca); ccs_l.append(cc); hs_l.append(hh)

    def to_pages(r):
        b_, d_, s_, h_, e_ = r.shape
        return r.reshape(b_, d_, s_ // ps, ps, h_, e_).transpose(0, 2, 1, 3, 4, 5)

    new_k_prompt = to_pages(jnp.stack(kp_l, axis=1))
    new_v_prompt = to_pages(jnp.stack(vp_l, axis=1))
    new_mem_k_prompt = jnp.stack(mkp_l, axis=1)
    new_mem_v_prompt = jnp.stack(mvp_l, axis=1)
    new_conv_a_prompt = jnp.stack(cap_l, axis=1)
    new_conv_c_prompt = jnp.stack(ccp_l, axis=1)
    new_ssm_prompt = jnp.stack(hp_l, axis=1)
    new_k_sample = jnp.stack(ks_l, axis=1)
    new_v_sample = jnp.stack(vs_l, axis=1)
    new_conv_a_sample = jnp.stack(cas_l, axis=1)
    new_conv_c_sample = jnp.stack(ccs_l, axis=1)
    new_ssm_sample = jnp.stack(hs_l, axis=1)
    return (xp, xs, new_k_prompt, new_v_prompt, new_mem_k_prompt, new_mem_v_prompt,
            new_conv_a_prompt, new_conv_c_prompt, new_ssm_prompt,
            new_k_sample, new_v_sample, new_conv_a_sample, new_conv_c_sample, new_ssm_sample)
```

```python
import functools
import math

import jax
import jax.numpy as jnp
from jax import lax
from jax.experimental import pallas as pl
from jax.experimental.pallas import tpu as pltpu

F32 = jnp.float32
BF16 = jnp.bfloat16

LANES = 128
SUBLANES = 8
VMEM_BYTES_V7X = 64 * 1024 * 1024
VMEM_LIMIT = (VMEM_BYTES_V7X * 3) // 4

BR_WIDTH = 1024
N_BRANCH = 4
CONV_A_WIDTH = 31
HEADS_B = 8
KV_HEADS_B = 4
GROUP_B = HEADS_B // KV_HEADS_B
HEAD_DIM_B = 64
V_DIM_B = 2 * HEAD_DIM_B
D_INNER_C = 1024
HEAD_DIM_C = 64
HEADS_C = D_INNER_C // HEAD_DIM_C
GROUPS_C = 4
D_STATE_C = 128
CONV_C_WIDTH = 4
CONV_C_DIM = D_INNER_C + 2 * GROUPS_C * D_STATE_C
SSD_CHUNK = 128
HEADS_M = 4
HEAD_DIM_M = BR_WIDTH // HEADS_M
EPS = 1e-5
NEG_INF = float("-inf")


def _tile(n, pref):
    if n <= pref:
        return n
    t = pref
    while n % t:
        t //= 2
    return t


def _sigmoid(x):
    return 1.0 / (1.0 + jnp.exp(-x))


def _silu(x):
    return x * _sigmoid(x)


def _params(sem):
    return pltpu.CompilerParams(dimension_semantics=sem, vmem_limit_bytes=VMEM_LIMIT)


def _proj_kernel(*refs, epilogue):
    x = refs[0][...]
    if epilogue == "glu":
        _, w1_ref, w2_ref, o_ref = refs
        a1 = jnp.dot(x, w1_ref[...], preferred_element_type=F32)
        a2 = jnp.dot(x, w2_ref[...], preferred_element_type=F32)
        o_ref[...] = (a1 * _sigmoid(a2)).astype(o_ref.dtype)
    elif epilogue == "sigmoid_bias":
        _, w_ref, b_ref, o_ref = refs
        y = jnp.dot(x, w_ref[...], preferred_element_type=F32)
        o_ref[...] = _sigmoid(y + b_ref[...]).astype(o_ref.dtype)
    elif epilogue == "dual":
        _, w_ref, o32_ref, o16_ref = refs
        y = jnp.dot(x, w_ref[...], preferred_element_type=F32)
        o32_ref[...] = y
        o16_ref[...] = y.astype(BF16)
    else:
        _, w_ref, o_ref = refs
        y = jnp.dot(x, w_ref[...], preferred_element_type=F32)
        if epilogue == "silu":
            y = _silu(y)
        o_ref[...] = y.astype(o_ref.dtype)


def _proj(x, w, *, epilogue, out_dtype=F32, bias=None, tn_pref=1024, tm_pref=512):
    m, k = x.shape
    n = w.shape[1] // 2 if epilogue == "glu" else w.shape[1]
    tm, tn = _tile(m, tm_pref), _tile(n, tn_pref)
    nj = n // tn
    x_spec = pl.BlockSpec((tm, k), lambda j, i: (i, 0))
    w_spec = pl.BlockSpec((k, tn), lambda j, i: (0, j))
    o_spec = pl.BlockSpec((tm, tn), lambda j, i: (i, j))
    in_specs, args = [x_spec, w_spec], [x, w]
    out_shape, out_specs = jax.ShapeDtypeStruct((m, n), out_dtype), o_spec
    if epilogue == "glu":
        in_specs.append(pl.BlockSpec((k, tn), lambda j, i: (0, j + nj)))
        args.append(w)
    elif epilogue == "sigmoid_bias":
        in_specs.append(pl.BlockSpec((1, tn), lambda j, i: (0, j)))
        args.append(bias)
    elif epilogue == "dual":
        out_shape = (jax.ShapeDtypeStruct((m, n), F32), jax.ShapeDtypeStruct((m, n), BF16))
        out_specs = (o_spec, o_spec)
    return pl.pallas_call(
        functools.partial(_proj_kernel, epilogue=epilogue),
        grid=(nj, m // tm),
        in_specs=in_specs,
        out_specs=out_specs,
        out_shape=out_shape,
        compiler_params=_params(("parallel", "parallel")),
        name="proj_" + epilogue,
    )(*args)


def _conv_kernel(*refs, taps, hist, tt, rb, post, has_state):
    refs = list(refs)
    x_ref = refs.pop(0)
    state_ref = refs.pop(0) if has_state else None
    w_ref, b_ref = refs.pop(0), refs.pop(0)
    if post == "ln_swish_gate":
        g_ref, beta_ref, gate_ref = refs.pop(0), refs.pop(0), refs.pop(0)
    y_ref, ns_ref, win_ref = refs
    t = pl.program_id(1)
    back = taps - 1
    c = x_ref.shape[-1]

    @pl.when(t == 0)
    def _():
        win_ref[0:hist, :] = jnp.zeros((hist, c), F32)
        if has_state:
            win_ref[hist - back:hist, :] = state_ref[0].astype(F32)

    win_ref[hist:hist + tt, :] = x_ref[0].astype(F32)

    for r0 in range(0, tt, rb):
        acc = jnp.broadcast_to(b_ref[...], (rb, c))
        for j in range(taps):
            acc = acc + w_ref[j:j + 1, :] * win_ref[r0 + j + hist - back:r0 + j + hist - back + rb, :]
        if post == "ln_swish_gate":
            mu = jnp.mean(acc, axis=-1, keepdims=True)
            d = acc - mu
            var = jnp.mean(d * d, axis=-1, keepdims=True)
            y = d * lax.rsqrt(var + EPS) * g_ref[...] + beta_ref[...]
            y = _silu(y) * gate_ref[0, r0:r0 + rb, :].astype(F32)
        else:
            y = _silu(acc)
        y_ref[0, r0:r0 + rb, :] = y.astype(y_ref.dtype)

    @pl.when(t == pl.num_programs(1) - 1)
    def _():
        ns_ref[0] = win_ref[hist + tt - back:hist + tt, :]

    win_ref[0:hist, :] = win_ref[tt:tt + hist, :]


def _conv(x, state, w, b, *, post, out_dtype, ln_g=None, ln_b=None, gate=None, gate_col=0, tt_pref=256):
    bsz, t, _ = x.shape
    taps, c = w.shape
    back = taps - 1
    hist = -(-back // SUBLANES) * SUBLANES
    tt = _tile(t, tt_pref)
    rb = min(tt, max(SUBLANES, (32 * 1024) // c))
    has_state = state is not None
    in_specs = [pl.BlockSpec((1, tt, c), lambda i, j: (i, j, 0))]
    args = [x]
    if has_state:
        in_specs.append(pl.BlockSpec((1, back, c), lambda i, j: (i, 0, 0)))
        args.append(state)
    in_specs += [pl.BlockSpec((taps, c), lambda i, j: (0, 0)), pl.BlockSpec((1, c), lambda i, j: (0, 0))]
    args += [w, b]
    if post == "ln_swish_gate":
        in_specs += [pl.BlockSpec((1, c), lambda i, j: (0, 0)), pl.BlockSpec((1, c), lambda i, j: (0, 0)),
                     pl.BlockSpec((1, tt, c), lambda i, j: (i, j, gate_col))]
        args += [ln_g, ln_b, gate]
    return pl.pallas_call(
        functools.partial(_conv_kernel, taps=taps, hist=hist, tt=tt, rb=rb, post=post, has_state=has_state),
        grid=(bsz, t // tt),
        in_specs=in_specs,
        out_specs=(pl.BlockSpec((1, tt, c), lambda i, j: (i, j, 0)),
                   pl.BlockSpec((1, back, c), lambda i, j: (i, 0, 0))),
        out_shape=(jax.ShapeDtypeStruct((bsz, t, c), out_dtype), jax.ShapeDtypeStruct((bsz, back, c), F32)),
        scratch_shapes=[pltpu.VMEM((hist + tt, c), F32)],
        compiler_params=_params(("parallel", "arbitrary")),
        name="conv_" + post,
    )(*args)


def _split3(x):
    hi = x.astype(BF16)
    r1 = x - hi.astype(F32)
    mid = r1.astype(BF16)
    lo = (r1 - mid.astype(F32)).astype(BF16)
    return hi, mid, lo


def _dot01(a01, x, *, left):
    out = None
    for piece in _split3(x):
        y = (jnp.dot(a01, piece, preferred_element_type=F32) if left
             else jnp.dot(piece, a01, preferred_element_type=F32))
        out = y if out is None else out + y
    return out


def _ssd_kernel(*refs, q, t_valid, has_h0):
    refs = list(refs)
    xc_ref, dt_ref, sz_ref = refs.pop(0), refs.pop(0), refs.pop(0)
    h0_ref = refs.pop(0) if has_h0 else None
    dtb_ref, alog_ref, e_ref, dskip_ref, nw_ref, y_ref, hout_ref, ht_ref = refs
    ci = pl.program_id(1)
    rows = xc_ref.shape[1]
    nstate = GROUPS_C * D_STATE_C

    @pl.when(ci == 0)
    def _():
        if has_h0:
            ht_ref[...] = h0_ref[0].reshape(D_INNER_C, D_STATE_C).T
        else:
            ht_ref[...] = jnp.zeros(ht_ref.shape, F32)

    xc = xc_ref[0]
    dtr = dt_ref[0]
    if rows < q:
        xc = jnp.concatenate([xc, jnp.zeros((q - rows, xc.shape[1]), F32)], axis=0)
        dtr = jnp.concatenate([dtr, jnp.zeros((q - rows, dtr.shape[1]), F32)], axis=0)
    x = xc[:, :D_INNER_C]
    bm = xc[:, D_INNER_C:D_INNER_C + nstate]
    cm = xc[:, D_INNER_C + nstate:]

    z = dtr + dtb_ref[...]
    dt = jnp.maximum(z, 0.0) + jnp.log1p(jnp.exp(-jnp.abs(z)))
    row_id = lax.broadcasted_iota(jnp.int32, (q, LANES), 0)
    if t_valid < q:
        dt = jnp.where(row_id < t_valid, dt, 0.0)
    a_neg = -jnp.exp(alog_ref[...])
    r_i = lax.broadcasted_iota(jnp.int32, (q, q), 0)
    c_i = lax.broadcasted_iota(jnp.int32, (q, q), 1)
    causal = r_i >= c_i
    tril = jnp.where(causal, 1.0, 0.0).astype(BF16)
    cum = _dot01(tril, dt * a_neg, left=True)
    cum_last = cum[q - 1:q, :]
    expand = e_ref[...]
    dt_e = _dot01(expand, dt, left=False)
    ecum_e = _dot01(expand, jnp.exp(cum), left=False)
    toend_e = _dot01(expand, jnp.exp(cum_last - cum), left=False)
    cum_t = cum.T
    xdt = x * dt_e
    xdt16 = xdt.astype(BF16)
    xdte16 = (xdt * toend_e).astype(BF16)
    lane = lax.broadcasted_iota(jnp.int32, (q, LANES), 1)

    hg = HEADS_C // GROUPS_C
    gw = hg * HEAD_DIM_C
    ys = []
    for g in range(GROUPS_C):
        bg = bm[:, g * D_STATE_C:(g + 1) * D_STATE_C]
        cg16 = cm[:, g * D_STATE_C:(g + 1) * D_STATE_C].astype(BF16)
        bgt16 = bg.T.astype(BF16)
        cb = jnp.dot(cg16, bgt16, preferred_element_type=F32)
        h_prev = ht_ref[:, g * gw:(g + 1) * gw]
        y_inter = jnp.dot(cg16, h_prev.astype(BF16), preferred_element_type=F32) * ecum_e[:, g * gw:(g + 1) * gw]
        pairs = []
        for k in range(hg // 2):
            xp = xdt16[:, g * gw + k * LANES:g * gw + (k + 1) * LANES]
            halves = []
            for hh in range(2):
                h = g * hg + 2 * k + hh
                seg = cum[:, h:h + 1] - cum_t[h:h + 1, :]
                dec = jnp.exp(jnp.where(causal, seg, NEG_INF))
                halves.append(jnp.dot((cb * dec).astype(BF16), xp, preferred_element_type=F32))
            pairs.append(jnp.where(lane < HEAD_DIM_C, halves[0], halves[1]))
        ys.append(jnp.concatenate(pairs, axis=1) + y_inter)
        ht_ref[:, g * gw:(g + 1) * gw] = (
            h_prev * ecum_e[q - 1:q, g * gw:(g + 1) * gw]
            + jnp.dot(bgt16, xdte16[:, g * gw:(g + 1) * gw], preferred_element_type=F32))
    y = jnp.concatenate(ys, axis=1) + dskip_ref[...] * x
    y = y * sz_ref[0].astype(F32) if rows == q else y[:rows] * sz_ref[0].astype(F32)
    y = y * lax.rsqrt(jnp.mean(y * y, axis=-1, keepdims=True) + EPS) * nw_ref[...]
    y_ref[0] = y.astype(y_ref.dtype)

    @pl.when(ci == pl.num_programs(1) - 1)
    def _():
        hout_ref[0] = ht_ref[...].T.reshape(HEADS_C, HEAD_DIM_C, D_STATE_C)


def _ssd(xc, dt_raw, dt_col, sz, sz_col, h0, dtb, alog, expand, dskip_e, normw):
    bsz, t, _ = xc.shape
    q = SSD_CHUNK
    rows = q if t % q == 0 else t
    assert rows <= q and rows % SUBLANES == 0
    nc = t // rows
    has_h0 = h0 is not None
    in_specs = [pl.BlockSpec((1, rows, CONV_C_DIM), lambda b, c: (b, c, 0)),
                pl.BlockSpec((1, rows, LANES), lambda b, c: (b, c, dt_col)),
                pl.BlockSpec((1, rows, D_INNER_C), lambda b, c: (b, c, sz_col))]
    args = [xc, dt_raw, sz]
    if has_h0:
        in_specs.append(pl.BlockSpec((1, HEADS_C, HEAD_DIM_C, D_STATE_C), lambda b, c: (b, 0, 0, 0)))
        args.append(h0)
    in_specs += [pl.BlockSpec((1, LANES), lambda b, c: (0, 0)), pl.BlockSpec((1, LANES), lambda b, c: (0, 0)),
                 pl.BlockSpec((LANES, D_INNER_C), lambda b, c: (0, 0)),
                 pl.BlockSpec((1, D_INNER_C), lambda b, c: (0, 0)), pl.BlockSpec((1, D_INNER_C), lambda b, c: (0, 0))]
    args += [dtb, alog, expand, dskip_e, normw]
    return pl.pallas_call(
        functools.partial(_ssd_kernel, q=q, t_valid=rows, has_h0=has_h0),
        grid=(bsz, nc),
        in_specs=in_specs,
        out_specs=(pl.BlockSpec((1, rows, D_INNER_C), lambda b, c: (b, c, 0)),
                   pl.BlockSpec((1, HEADS_C, HEAD_DIM_C, D_STATE_C), lambda b, c: (b, 0, 0, 0))),
        out_shape=(jax.ShapeDtypeStruct((bsz, t, D_INNER_C), BF16),
                   jax.ShapeDtypeStruct((bsz, HEADS_C, HEAD_DIM_C, D_STATE_C), F32)),
        scratch_shapes=[pltpu.VMEM((D_STATE_C, D_INNER_C), F32)],
        compiler_params=_params(("parallel", "arbitrary")),
        name="ssd",
    )(*args)


def _lam(lamp_ref, lam_init):
    p = lamp_ref[...]
    s1 = jnp.sum(p[0:1] * p[1:2], axis=-1, keepdims=True)
    s2 = jnp.sum(p[2:3] * p[3:4], axis=-1, keepdims=True)
    return jnp.exp(s1) - jnp.exp(s2) + lam_init


def _diff_finish(o0, o1, lam, subw, lam_init, gate):
    o = o0 - lam * o1
    o = o * lax.rsqrt(jnp.mean(o * o, axis=-1, keepdims=True) + EPS) * subw
    return o * (1.0 - lam_init) * gate


def _online_softmax_step(s, v16, m_ref, l_ref, acc_ref):
    m_prev = m_ref[...]
    m_new = jnp.maximum(m_prev, jnp.max(s, axis=-1, keepdims=True))
    p = jnp.exp(s - m_new)
    alpha = jnp.exp(m_prev - m_new)
    l_ref[...] = alpha * l_ref[...] + jnp.sum(p, axis=-1, keepdims=True)
    acc_ref[...] = alpha * acc_ref[...] + jnp.dot(p.astype(BF16), v16, preferred_element_type=F32)
    m_ref[...] = m_new


def _attn_prompt_kernel(q_ref, k_ref, v_ref, gate_ref, lamp_ref, subw_ref, o_ref,
                        qs_ref, m_ref, l_ref, acc_ref, *, tq, lam_init):
    qi = pl.program_id(2)
    lane = lax.broadcasted_iota(jnp.int32, (tq, LANES), 1)
    for g in range(GROUP_B):
        qg = q_ref[0, :, g * LANES:(g + 1) * LANES]
        for c in range(2):
            keep = (lane >= HEAD_DIM_B) if c else (lane < HEAD_DIM_B)
            qs_ref[(2 * g + c) * tq:(2 * g + c + 1) * tq, :] = jnp.where(keep, qg, jnp.zeros_like(qg))
    m_ref[...] = jnp.full(m_ref.shape, NEG_INF, F32)
    l_ref[...] = jnp.zeros(l_ref.shape, F32)
    acc_ref[...] = jnp.zeros(acc_ref.shape, F32)

    def step(ki, masked):
        start = pl.multiple_of(ki * tq, tq)
        kt = k_ref[0, pl.ds(start, tq), :]
        vt = v_ref[0, pl.ds(start, tq), :]
        s = lax.dot_general(qs_ref[...], kt, (((1,), (1,)), ((), ())), preferred_element_type=F32)
        if masked:
            r_i = lax.broadcasted_iota(jnp.int32, (tq, tq), 0)
            c_i = lax.broadcasted_iota(jnp.int32, (tq, tq), 1)
            keep = r_i >= c_i
            s = jnp.concatenate([jnp.where(keep, s[i * tq:(i + 1) * tq], NEG_INF) for i in range(2 * GROUP_B)],
                                axis=0)
        _online_softmax_step(s, vt, m_ref, l_ref, acc_ref)

    def body(ki, carry):
        step(ki, False)
        return carry

    lax.fori_loop(0, qi, body, 0)
    step(qi, True)

    lam = _lam(lamp_ref, lam_init)
    for g in range(GROUP_B):
        r0, r1 = (2 * g) * tq, (2 * g + 1) * tq
        o0 = acc_ref[r0:r0 + tq, :] / l_ref[r0:r0 + tq, :]
        o1 = acc_ref[r1:r1 + tq, :] / l_ref[r1:r1 + tq, :]
        gate = gate_ref[0, :, g * LANES:(g + 1) * LANES].astype(F32)
        o_ref[0, :, g * LANES:(g + 1) * LANES] = _diff_finish(
            o0, o1, lam, subw_ref[...], lam_init, gate).astype(o_ref.dtype)


def _attn_prompt(qm, kv16, silu_all, lamp, subw, lam_init, *, tq_pref=256):
    bsz, s, _ = kv16.shape
    tq = _tile(s, tq_pref)
    hw = GROUP_B * V_DIM_B
    gate_blk0 = BR_WIDTH // hw
    return pl.pallas_call(
        functools.partial(_attn_prompt_kernel, tq=tq, lam_init=lam_init),
        grid=(bsz, KV_HEADS_B, s // tq),
        in_specs=[pl.BlockSpec((1, tq, hw), lambda b, h, i: (b, i, h)),
                  pl.BlockSpec((1, s, V_DIM_B), lambda b, h, i: (b, 0, h)),
                  pl.BlockSpec((1, s, V_DIM_B), lambda b, h, i: (b, 0, KV_HEADS_B + h)),
                  pl.BlockSpec((1, tq, hw), lambda b, h, i: (b, i, gate_blk0 + h)),
                  pl.BlockSpec((4, HEAD_DIM_B), lambda b, h, i: (0, 0)),
                  pl.BlockSpec((1, V_DIM_B), lambda b, h, i: (0, 0))],
        out_specs=pl.BlockSpec((1, tq, hw), lambda b, h, i: (b, i, h)),
        out_shape=jax.ShapeDtypeStruct((bsz, s, BR_WIDTH), BF16),
        scratch_shapes=[pltpu.VMEM((2 * GROUP_B * tq, LANES), BF16),
                        pltpu.VMEM((2 * GROUP_B * tq, 1), F32),
                        pltpu.VMEM((2 * GROUP_B * tq, 1), F32),
                        pltpu.VMEM((2 * GROUP_B * tq, V_DIM_B), F32)],
        compiler_params=_params(("parallel", "parallel", "arbitrary")),
        name="attn_prompt",
    )(qm, kv16, kv16, silu_all, lamp, subw)


def _attn_sample_kernel(*refs, pps, t_new, lam_init):
    pt_ref = refs[0]
    del pt_ref
    qb_ref = refs[1]
    k_refs = refs[2:2 + pps]
    v_refs = refs[2 + pps:2 + 2 * pps]
    kn_ref, vn_ref, gate_ref, lamp_ref, subw_ref, o_ref, m_ref, l_ref, acc_ref = refs[2 + 2 * pps:]
    si = pl.program_id(1)
    nrow = qb_ref.shape[1]
    kvw = KV_HEADS_B * V_DIM_B

    @pl.when(si == 0)
    def _():
        m_ref[...] = jnp.full(m_ref.shape, NEG_INF, F32)
        l_ref[...] = jnp.zeros(l_ref.shape, F32)
        acc_ref[...] = jnp.zeros(acc_ref.shape, F32)

    qb = qb_ref[0]
    nt = (((1,), (1,)), ((), ()))
    k16 = jnp.concatenate([r[0, 0].astype(BF16) for r in k_refs], axis=0)
    v16 = jnp.concatenate([r[0, 0].astype(BF16) for r in v_refs], axis=0)
    s = lax.dot_general(qb, k16, nt, preferred_element_type=F32)
    _online_softmax_step(s, v16, m_ref, l_ref, acc_ref)

    @pl.when(si == pl.num_programs(1) - 1)
    def _():
        pad = jnp.zeros((LANES - t_new, kvw), BF16)
        kn = jnp.concatenate([kn_ref[0].astype(BF16), pad], axis=0)
        vn = jnp.concatenate([vn_ref[0].astype(BF16), pad], axis=0)
        sn = lax.dot_general(qb, kn, nt, preferred_element_type=F32)
        tok = lax.broadcasted_iota(jnp.int32, (nrow, LANES), 0) % t_new
        col = lax.broadcasted_iota(jnp.int32, (nrow, LANES), 1)
        sn = jnp.where(col <= tok, sn, NEG_INF)
        _online_softmax_step(sn, vn, m_ref, l_ref, acc_ref)
        lam = _lam(lamp_ref, lam_init)
        for h in range(KV_HEADS_B):
            for g in range(GROUP_B):
                r0 = ((h * 2 + 0) * GROUP_B + g) * t_new
                r1 = ((h * 2 + 1) * GROUP_B + g) * t_new
                cs = slice(h * V_DIM_B, (h + 1) * V_DIM_B)
                o0 = acc_ref[r0:r0 + t_new, cs] / l_ref[r0:r0 + t_new, :]
                o1 = acc_ref[r1:r1 + t_new, cs] / l_ref[r1:r1 + t_new, :]
                oc = slice((h * GROUP_B + g) * V_DIM_B, (h * GROUP_B + g + 1) * V_DIM_B)
                gate = gate_ref[0, :, oc].astype(F32)
                o_ref[0, :, oc] = _diff_finish(o0, o1, lam, subw_ref[...], lam_init, gate).astype(o_ref.dtype)


def _attn_sample(qblk, cache_k, cache_v, page_table, layer, kv32, silu_all, lamp, subw, lam_init):
    db, nrow, kvw = qblk.shape
    t_new = kv32.shape[1]
    page = cache_k.shape[2]
    n_pages = page_table.shape[1]
    pps = _tile(n_pages, 8)
    steps = n_pages // pps

    def page_spec(i):
        return pl.BlockSpec((1, 1, page, kvw), lambda b, s, pt: (pt[b, s * pps + i], layer, 0, 0))

    in_specs = [pl.BlockSpec((1, nrow, kvw), lambda b, s, pt: (b, 0, 0))]
    in_specs += [page_spec(i) for i in range(pps)] + [page_spec(i) for i in range(pps)]
    in_specs += [pl.BlockSpec((1, t_new, kvw), lambda b, s, pt: (b, 0, 0)),
                 pl.BlockSpec((1, t_new, kvw), lambda b, s, pt: (b, 0, 1)),
                 pl.BlockSpec((1, t_new, BR_WIDTH), lambda b, s, pt: (b, 0, 1)),
                 pl.BlockSpec((4, HEAD_DIM_B), lambda b, s, pt: (0, 0)),
                 pl.BlockSpec((1, V_DIM_B), lambda b, s, pt: (0, 0))]
    grid_spec = pltpu.PrefetchScalarGridSpec(
        num_scalar_prefetch=1,
        grid=(db, steps),
        in_specs=in_specs,
        out_specs=pl.BlockSpec((1, t_new, BR_WIDTH), lambda b, s, pt: (b, 0, 0)),
        scratch_shapes=[pltpu.VMEM((nrow, 1), F32), pltpu.VMEM((nrow, 1), F32), pltpu.VMEM((nrow, kvw), F32)])
    return pl.pallas_call(
        functools.partial(_attn_sample_kernel, pps=pps, t_new=t_new, lam_init=lam_init),
        grid_spec=grid_spec,
        out_shape=jax.ShapeDtypeStruct((db, t_new, BR_WIDTH), BF16),
        compiler_params=_params(("parallel", "arbitrary")),
        name="attn_sample",
    )(page_table, qblk, *([cache_k] * pps), *([cache_v] * pps), kv32, kv32, silu_all, lamp, subw)


def _mem_attn_kernel(q_ref, mk_ref, mv_ref, gate_ref, o_ref):
    nt = (((1,), (1,)), ((), ()))
    for h in range(HEADS_M):
        cs = slice(h * HEAD_DIM_M, (h + 1) * HEAD_DIM_M)
        s = lax.dot_general(q_ref[0, :, cs], mk_ref[0, 0, :, cs].astype(BF16), nt, preferred_element_type=F32)
        p = jnp.exp(s - jnp.max(s, axis=-1, keepdims=True))
        p = p / jnp.sum(p, axis=-1, keepdims=True)
        o = jnp.dot(p.astype(BF16), mv_ref[0, 0, :, cs].astype(BF16), preferred_element_type=F32)
        o_ref[0, :, cs] = (o * gate_ref[0, :, cs].astype(F32)).astype(o_ref.dtype)


def _mem_attn(qm, mk, mk_idx, mv, mv_idx, silu_all, *, tq_pref=512):
    bsz, t, _ = qm.shape
    n_mem = mk.shape[2]
    tq = _tile(t, tq_pref)
    return pl.pallas_call(
        _mem_attn_kernel,
        grid=(bsz, t // tq),
        in_specs=[pl.BlockSpec((1, tq, BR_WIDTH), lambda b, i: (b, i, 1)),
                  pl.BlockSpec((1, 1, n_mem, BR_WIDTH), lambda b, i: (b, mk_idx[0], 0, mk_idx[1])),
                  pl.BlockSpec((1, 1, n_mem, BR_WIDTH), lambda b, i: (b, mv_idx[0], 0, mv_idx[1])),
                  pl.BlockSpec((1, tq, BR_WIDTH), lambda b, i: (b, i, 3))],
        out_specs=pl.BlockSpec((1, tq, BR_WIDTH), lambda b, i: (b, i, 0)),
        out_shape=jax.ShapeDtypeStruct((bsz, t, BR_WIDTH), BF16),
        compiler_params=_params(("parallel", "parallel")),
        name="mem_attn",
    )(qm, mk, mv, silu_all)


def _merge_kernel(ya_ref, yb_ref, yc_ref, ym_ref, gates_ref, wb_ref, wo_ref, x_ref, g_ref, b_ref,
                  o32_ref, o16_ref, *, alpha):
    d = x_ref.shape[-1]
    mix = None
    for k, y_ref in enumerate((ya_ref, yb_ref, yc_ref, ym_ref)):
        pk = jnp.dot(y_ref[...], wb_ref[k], preferred_element_type=F32)
        pk = pk * gates_ref[:, k * d:(k + 1) * d].astype(F32)
        mix = pk if mix is None else mix + pk
    out = jnp.dot(mix.astype(BF16), wo_ref[...], preferred_element_type=F32)
    z = alpha * x_ref[...] + out
    mu = jnp.mean(z, axis=-1, keepdims=True)
    zc = z - mu
    var = jnp.mean(zc * zc, axis=-1, keepdims=True)
    y = zc * lax.rsqrt(var + EPS) * g_ref[...] + b_ref[...]
    o32_ref[...] = y
    o16_ref[...] = y.astype(BF16)


def _merge(ya, yb, yc, ym, gates, wb, wo, x, ln_g, ln_b, alpha, *, tm_pref=256):
    m, d = x.shape
    tm = _tile(m, tm_pref)
    row = lambda w: pl.BlockSpec((tm, w), lambda i: (i, 0))
    const = lambda shape: pl.BlockSpec(shape, lambda i: (0,) * len(shape))
    return pl.pallas_call(
        functools.partial(_merge_kernel, alpha=alpha),
        grid=(m // tm,),
        in_specs=[row(BR_WIDTH), row(BR_WIDTH), row(BR_WIDTH), row(BR_WIDTH), row(N_BRANCH * d),
                  const((N_BRANCH, BR_WIDTH, d)), const((d, d)), row(d), const((1, d)), const((1, d))],
        out_specs=(row(d), row(d)),
        out_shape=(jax.ShapeDtypeStruct((m, d), F32), jax.ShapeDtypeStruct((m, d), BF16)),
        compiler_params=_params(("parallel",)),
        name="merge",
    )(ya, yb, yc, ym, gates, wb, wo, x, ln_g, ln_b)


def _layer_weights(l, w_in, b_gate, conv_a_w, conv_a_b, ln_a_g, ln_a_b, lam_q1, lam_k1, lam_q2, lam_k2, subln_w,
                   conv_c_w, conv_c_b, dt_bias, a_log, d_skip, norm_c_w, w_mem_kv, w_branch, w_out, ln_g, ln_b):
    w = w_in[l]
    off, cols = 0, {}
    for name, n in (("a_glu", 2 * BR_WIDTH), ("a_gate", BR_WIDTH), ("b_q", BR_WIDTH), ("b_k", KV_HEADS_B * V_DIM_B),
                    ("b_v", KV_HEADS_B * V_DIM_B), ("b_gate", BR_WIDTH), ("c_z", D_INNER_C), ("c_xbc", CONV_C_DIM),
                    ("c_dt", HEADS_C), ("m_q", BR_WIDTH), ("m_gate", BR_WIDTH), ("g", N_BRANCH * w.shape[0])):
        cols[name] = w[:, off:off + n]
        off += n
    d = w.shape[0]
    pad_h = LANES - HEADS_C
    row = lambda v: v.reshape(1, -1).astype(F32)
    return dict(
        w_glu=cols["a_glu"].astype(BF16),
        w_silu=jnp.concatenate([cols["a_gate"], cols["b_gate"], cols["c_z"], cols["m_gate"]], axis=1).astype(BF16),
        w_qm=jnp.concatenate([cols["b_q"] * HEAD_DIM_B ** -0.5, cols["m_q"] * HEAD_DIM_M ** -0.5], axis=1).astype(BF16),
        w_kv=jnp.concatenate([cols["b_k"], cols["b_v"]], axis=1).astype(BF16),
        w_xdt=jnp.concatenate([cols["c_xbc"], cols["c_dt"], jnp.zeros((d, pad_h), F32)], axis=1).astype(BF16),
        w_g=cols["g"].astype(BF16),
        b_gate=row(b_gate[l]),
        conv_a_w=conv_a_w[l], conv_a_b=row(conv_a_b[l]), ln_a_g=row(ln_a_g[l]), ln_a_b=row(ln_a_b[l]),
        lamp=jnp.stack([lam_q1[l], lam_k1[l], lam_q2[l], lam_k2[l]]).astype(F32),
        subln_w=row(subln_w[l]),
        conv_c_w=conv_c_w[l], conv_c_b=row(conv_c_b[l]),
        dtb=row(jnp.pad(dt_bias[l], (0, pad_h))), alog=row(jnp.pad(a_log[l], (0, pad_h))),
        dskip_e=row(jnp.repeat(d_skip[l], HEAD_DIM_C)), norm_c_w=row(norm_c_w[l]),
        w_mem_kv=w_mem_kv[l].astype(BF16),
        w_branch=w_branch[l].astype(BF16), w_out=w_out[l].astype(BF16),
        ln_g=row(ln_g[l]), ln_b=row(ln_b[l]),
        lam_init=0.8 - 0.6 * math.exp(-0.3 * l),
    )


def _layer(x32, x16, lw, attend, mem, conv_a_state, conv_c_state, ssm_state, expand, alpha):
    bsz, t, d = x32.shape
    m = bsz * t
    xf = x16.reshape(m, d)
    u = _proj(xf, lw["w_glu"], epilogue="glu", out_dtype=F32).reshape(bsz, t, BR_WIDTH)
    silu_all = _proj(xf, lw["w_silu"], epilogue="silu", out_dtype=BF16).reshape(bsz, t, 4 * BR_WIDTH)
    qm = _proj(xf, lw["w_qm"], epilogue="none", out_dtype=BF16).reshape(bsz, t, 2 * BR_WIDTH)
    kv32, kv16 = _proj(xf, lw["w_kv"], epilogue="dual")
    xdt = _proj(xf, lw["w_xdt"], epilogue="none", out_dtype=F32, tn_pref=CONV_C_DIM + LANES)
    gates = _proj(xf, lw["w_g"], epilogue="sigmoid_bias", out_dtype=BF16, bias=lw["b_gate"])
    kv32 = kv32.reshape(bsz, t, -1)
    kv16 = kv16.reshape(bsz, t, -1)
    xdt = xdt.reshape(bsz, t, CONV_C_DIM + LANES)

    y_a, new_conv_a = _conv(u, conv_a_state, lw["conv_a_w"], lw["conv_a_b"], post="ln_swish_gate", out_dtype=BF16,
                            ln_g=lw["ln_a_g"], ln_b=lw["ln_a_b"], gate=silu_all, gate_col=0)
    y_b = attend(qm, kv32, kv16, silu_all, lw)
    xc, new_conv_c = _conv(xdt, conv_c_state, lw["conv_c_w"], lw["conv_c_b"], post="silu", out_dtype=F32)
    y_c, new_h = _ssd(xc, xdt, CONV_C_DIM // LANES, silu_all, 2, ssm_state, lw["dtb"], lw["alog"], expand,
                      lw["dskip_e"], lw["norm_c_w"])
    y_m = _mem_attn(qm, mem[0], mem[1], mem[2], mem[3], silu_all)
    flat = lambda a: a.reshape(m, -1)
    x32n, x16n = _merge(flat(y_a), flat(y_b), flat(y_c), flat(y_m), gates, lw["w_branch"], lw["w_out"],
                        x32.reshape(m, d), lw["ln_g"], lw["ln_b"], alpha)
    return x32n.reshape(bsz, t, d), x16n.reshape(bsz, t, d), kv32, new_conv_a, new_conv_c, new_h


def kernel(x_prompt, x_sample, mem_prompt, cache_k, cache_v, page_table, cache_mem_k, cache_mem_v, state_conv_a, state_conv_c, state_ssm, w_in, b_gate, conv_a_w, conv_a_b, ln_a_g, ln_a_b, lam_q1, lam_k1, lam_q2, lam_k2, subln_w, conv_c_w, conv_c_b, dt_bias, a_log, d_skip, norm_c_w, w_mem_kv, w_branch, w_out, ln_g, ln_b):
    bsz, seq, d = x_prompt.shape
    db, t_new, _ = x_sample.shape
    depth = w_in.shape[0]
    n_phys, _, page, _, _ = cache_k.shape
    n_mem = mem_prompt.shape[1]
    alpha = (2 * depth) ** 0.25
    kvw = KV_HEADS_B * V_DIM_B

    ck = cache_k.reshape(n_phys, depth, page, kvw)
    cv = cache_v.reshape(n_phys, depth, page, kvw)
    cmk = cache_mem_k.reshape(db, depth, n_mem, BR_WIDTH)
    cmv = cache_mem_v.reshape(db, depth, n_mem, BR_WIDTH)
    mem16 = mem_prompt.reshape(bsz * n_mem, d).astype(BF16)
    head_of_lane = jnp.arange(D_INNER_C, dtype=jnp.int32) // HEAD_DIM_C
    expand = (jnp.arange(LANES, dtype=jnp.int32)[:, None] == head_of_lane[None, :]).astype(BF16)
    blk_of_row = jnp.arange(2 * KV_HEADS_B * GROUP_B * t_new, dtype=jnp.int32) // (GROUP_B * t_new)
    blk_of_col = jnp.arange(kvw, dtype=jnp.int32) // HEAD_DIM_B
    qmask = (blk_of_row[:, None] == blk_of_col[None, :])

    xp32, xp16 = x_prompt, x_prompt.astype(BF16)
    xs32, xs16 = x_sample, x_sample.astype(BF16)
    outs = {k: [] for k in ("kp", "vp", "mkp", "mvp", "cap", "ccp", "hp", "ks", "vs", "cas", "ccs", "hs")}
    for l in range(depth):
        lw = _layer_weights(l, w_in, b_gate, conv_a_w, conv_a_b, ln_a_g, ln_a_b, lam_q1, lam_k1, lam_q2, lam_k2,
                            subln_w, conv_c_w, conv_c_b, dt_bias, a_log, d_skip, norm_c_w, w_mem_kv, w_branch,
                            w_out, ln_g, ln_b)
        mkv32, mkv16 = _proj(mem16, lw["w_mem_kv"], epilogue="dual")
        mkv16 = mkv16.reshape(bsz, 1, n_mem, 2 * BR_WIDTH)

        def attend_prompt(qm, kv32, kv16, silu_all, lw):
            return _attn_prompt(qm, kv16, silu_all, lw["lamp"], lw["subln_w"], lw["lam_init"])

        xp32, xp16, kv32, ca, cc, hh = _layer(xp32, xp16, lw, attend_prompt, (mkv16, (0, 0), mkv16, (0, 1)),
                                               None, None, None, expand, alpha)
        outs["kp"].append(kv32[..., :kvw].reshape(bsz, seq // page, page, KV_HEADS_B, V_DIM_B))
        outs["vp"].append(kv32[..., kvw:].reshape(bsz, seq // page, page, KV_HEADS_B, V_DIM_B))
        mkv32 = mkv32.reshape(bsz, n_mem, 2 * BR_WIDTH)
        outs["mkp"].append(mkv32[..., :BR_WIDTH].reshape(bsz, n_mem, HEADS_M, HEAD_DIM_M))
        outs["mvp"].append(mkv32[..., BR_WIDTH:].reshape(bsz, n_mem, HEADS_M, HEAD_DIM_M))
        outs["cap"].append(ca); outs["ccp"].append(cc); outs["hp"].append(hh)

        def attend_sample(qm, kv32, kv16, silu_all, lw, l=l):
            q = qm[..., :BR_WIDTH].reshape(db, t_new, KV_HEADS_B, GROUP_B, 2, HEAD_DIM_B)
            q = q.transpose(0, 2, 4, 3, 1, 5).reshape(db, 2 * KV_HEADS_B * GROUP_B * t_new, 1, HEAD_DIM_B)
            qblk = jnp.where(qmask.reshape(1, -1, 2 * KV_HEADS_B, HEAD_DIM_B), q, jnp.zeros((), BF16))
            qblk = qblk.reshape(db, -1, kvw)
            return _attn_sample(qblk, ck, cv, page_table, l, kv32, silu_all, lw["lamp"], lw["subln_w"],
                                lw["lam_init"])

        xs32, xs16, kv32, ca, cc, hh = _layer(xs32, xs16, lw, attend_sample, (cmk, (l, 0), cmv, (l, 0)),
                                               state_conv_a[:, l], state_conv_c[:, l], state_ssm[:, l],
                                               expand, alpha)
        outs["ks"].append(kv32[..., :kvw].reshape(db, t_new, KV_HEADS_B, V_DIM_B))
        outs["vs"].append(kv32[..., kvw:].reshape(db, t_new, KV_HEADS_B, V_DIM_B))
        outs["cas"].append(ca); outs["ccs"].append(cc); outs["hs"].append(hh)

    st = lambda k, axis: jnp.stack(outs[k], axis=axis)
    return (xp32, xs32, st("kp", 2), st("vp", 2), st("mkp", 1), st("mvp", 1), st("cap", 1), st("ccp", 1),
            st("hp", 1), st("ks", 1), st("vs", 1), st("cas", 1), st("ccs", 1), st("hs", 1))
```

```python
import functools
import math

import jax
import jax.numpy as jnp
from jax import lax
from jax.experimental import pallas as pl
from jax.experimental.pallas import tpu as pltpu

F32 = jnp.float32
BF16 = jnp.bfloat16

LANES = 128
SUBLANES = 8
VMEM_BYTES_V7X = 64 * 1024 * 1024
VMEM_LIMIT = (VMEM_BYTES_V7X * 3) // 4

BR_WIDTH = 1024
N_BRANCH = 4
CONV_A_WIDTH = 31
HEADS_B = 8
KV_HEADS_B = 4
GROUP_B = HEADS_B // KV_HEADS_B
HEAD_DIM_B = 64
V_DIM_B = 2 * HEAD_DIM_B
D_INNER_C = 1024
HEAD_DIM_C = 64
HEADS_C = D_INNER_C // HEAD_DIM_C
GROUPS_C = 4
D_STATE_C = 128
CONV_C_WIDTH = 4
CONV_C_DIM = D_INNER_C + 2 * GROUPS_C * D_STATE_C
SSD_CHUNK = 128
HEADS_M = 4
HEAD_DIM_M = BR_WIDTH // HEADS_M
EPS = 1e-5
NEG_INF = float("-inf")
LOG2E = math.log2(math.e)


def _tile(n, pref):
    if n <= pref:
        return n
    t = pref
    while n % t:
        t //= 2
    return t


def _sigmoid(x):
    return 1.0 / (1.0 + jnp.exp(-x))


def _silu(x):
    return x * _sigmoid(x)


def _params(sem):
    return pltpu.CompilerParams(dimension_semantics=sem, vmem_limit_bytes=VMEM_LIMIT)


def _proj_kernel(*refs, epilogue):
    x = refs[0][...]
    if epilogue == "glu":
        _, w1_ref, w2_ref, o_ref = refs
        a1 = jnp.dot(x, w1_ref[...], preferred_element_type=F32)
        a2 = jnp.dot(x, w2_ref[...], preferred_element_type=F32)
        o_ref[...] = (a1 * _sigmoid(a2)).astype(o_ref.dtype)
    elif epilogue == "sigmoid_bias":
        _, w_ref, b_ref, o_ref = refs
        y = jnp.dot(x, w_ref[...], preferred_element_type=F32)
        o_ref[...] = _sigmoid(y + b_ref[...]).astype(o_ref.dtype)
    elif epilogue == "dual":
        _, w_ref, o32_ref, o16_ref = refs
        y = jnp.dot(x, w_ref[...], preferred_element_type=F32)
        o32_ref[...] = y
        o16_ref[...] = y.astype(BF16)
    else:
        _, w_ref, o_ref = refs
        y = jnp.dot(x, w_ref[...], preferred_element_type=F32)
        if epilogue == "silu":
            y = _silu(y)
        o_ref[...] = y.astype(o_ref.dtype)


def _proj(x, w, *, epilogue, out_dtype=F32, bias=None, tn_pref=1024, tm_pref=1024):
    m, k = x.shape
    n = w.shape[1] // 2 if epilogue == "glu" else w.shape[1]
    tm, tn = _tile(m, tm_pref), _tile(n, tn_pref)
    nj = n // tn
    x_spec = pl.BlockSpec((tm, k), lambda j, i: (i, 0))
    w_spec = pl.BlockSpec((k, tn), lambda j, i: (0, j))
    o_spec = pl.BlockSpec((tm, tn), lambda j, i: (i, j))
    in_specs, args = [x_spec, w_spec], [x, w]
    out_shape, out_specs = jax.ShapeDtypeStruct((m, n), out_dtype), o_spec
    if epilogue == "glu":
        in_specs.append(pl.BlockSpec((k, tn), lambda j, i: (0, j + nj)))
        args.append(w)
    elif epilogue == "sigmoid_bias":
        in_specs.append(pl.BlockSpec((1, tn), lambda j, i: (0, j)))
        args.append(bias)
    elif epilogue == "dual":
        out_shape = (jax.ShapeDtypeStruct((m, n), F32), jax.ShapeDtypeStruct((m, n), BF16))
        out_specs = (o_spec, o_spec)
    return pl.pallas_call(
        functools.partial(_proj_kernel, epilogue=epilogue),
        grid=(nj, m // tm),
        in_specs=in_specs,
        out_specs=out_specs,
        out_shape=out_shape,
        compiler_params=_params(("parallel", "parallel")),
        name="proj_" + epilogue,
    )(*args)


def _conv_kernel(*refs, taps, hist, tt, rb, post, has_state):
    refs = list(refs)
    x_ref = refs.pop(0)
    state_ref = refs.pop(0) if has_state else None
    w_ref, b_ref = refs.pop(0), refs.pop(0)
    if post == "ln_swish_gate":
        g_ref, beta_ref, gate_ref = refs.pop(0), refs.pop(0), refs.pop(0)
    y_ref, ns_ref, win_ref, sh_ref = refs
    t = pl.program_id(1)
    back = taps - 1
    c = x_ref.shape[-1]
    n_win = hist + tt

    @pl.when(t == 0)
    def _():
        win_ref[0:hist, :] = jnp.zeros((hist, c), F32)
        win_ref[n_win:n_win + SUBLANES, :] = jnp.zeros((SUBLANES, c), F32)
        if has_state:
            win_ref[hist - back:hist, :] = state_ref[0].astype(F32)

    win_ref[hist:n_win, :] = x_ref[0].astype(F32)

    offsets = [j + hist - back for j in range(taps)]
    for p in sorted({o % SUBLANES for o in offsets} - {0}):
        sh_ref[p - 1, :, :] = win_ref[p:p + n_win, :]

    for r0 in range(0, tt, rb):
        acc = jnp.broadcast_to(b_ref[...], (rb, c))
        for j, o in enumerate(offsets):
            a, p = divmod(o, SUBLANES)
            lo = r0 + a * SUBLANES
            xs = win_ref[lo:lo + rb, :] if p == 0 else sh_ref[p - 1, lo:lo + rb, :]
            wj = w_ref[j * SUBLANES:(j + 1) * SUBLANES, :]
            acc = acc + (wj if rb == SUBLANES else jnp.concatenate([wj] * (rb // SUBLANES), axis=0)) * xs
        if post == "ln_swish_gate":
            mu = jnp.mean(acc, axis=-1, keepdims=True)
            d = acc - mu
            var = jnp.mean(d * d, axis=-1, keepdims=True)
            y = d * lax.rsqrt(var + EPS) * g_ref[...] + beta_ref[...]
            y = _silu(y) * gate_ref[0, r0:r0 + rb, :].astype(F32)
        else:
            y = _silu(acc)
        y_ref[0, r0:r0 + rb, :] = y.astype(y_ref.dtype)

    @pl.when(t == pl.num_programs(1) - 1)
    def _():
        ns_ref[0] = win_ref[n_win - back:n_win, :]

    win_ref[0:hist, :] = win_ref[tt:tt + hist, :]


def _conv(x, state, w, b, *, post, out_dtype, ln_g=None, ln_b=None, gate=None, gate_col=0, tt_pref=256):
    bsz, t, _ = x.shape
    taps, c = w.shape
    back = taps - 1
    hist = -(-back // SUBLANES) * SUBLANES
    tt = _tile(t, tt_pref)
    rb = min(tt, max(SUBLANES, (32 * 1024) // c))
    has_state = state is not None
    in_specs = [pl.BlockSpec((1, tt, c), lambda i, j: (i, j, 0))]
    args = [x]
    if has_state:
        in_specs.append(pl.BlockSpec((1, back, c), lambda i, j: (i, 0, 0)))
        args.append(state)
    in_specs += [pl.BlockSpec((taps * SUBLANES, c), lambda i, j: (0, 0)), pl.BlockSpec((1, c), lambda i, j: (0, 0))]
    args += [jnp.repeat(w, SUBLANES, axis=0), b]
    if post == "ln_swish_gate":
        in_specs += [pl.BlockSpec((1, c), lambda i, j: (0, 0)), pl.BlockSpec((1, c), lambda i, j: (0, 0)),
                     pl.BlockSpec((1, tt, c), lambda i, j: (i, j, gate_col))]
        args += [ln_g, ln_b, gate]
    return pl.pallas_call(
        functools.partial(_conv_kernel, taps=taps, hist=hist, tt=tt, rb=rb, post=post, has_state=has_state),
        grid=(bsz, t // tt),
        in_specs=in_specs,
        out_specs=(pl.BlockSpec((1, tt, c), lambda i, j: (i, j, 0)),
                   pl.BlockSpec((1, back, c), lambda i, j: (i, 0, 0))),
        out_shape=(jax.ShapeDtypeStruct((bsz, t, c), out_dtype), jax.ShapeDtypeStruct((bsz, back, c), F32)),
        scratch_shapes=[pltpu.VMEM((hist + tt + SUBLANES, c), F32),
                        pltpu.VMEM((SUBLANES - 1, hist + tt, c), F32)],
        compiler_params=_params(("parallel", "arbitrary")),
        name="conv_" + post,
    )(*args)


def _split3(x):
    hi = x.astype(BF16)
    r1 = x - hi.astype(F32)
    mid = r1.astype(BF16)
    lo = (r1 - mid.astype(F32)).astype(BF16)
    return hi, mid, lo


def _dot01(a01, x, *, left):
    out = None
    for piece in _split3(x):
        y = (jnp.dot(a01, piece, preferred_element_type=F32) if left
             else jnp.dot(piece, a01, preferred_element_type=F32))
        out = y if out is None else out + y
    return out


def _ssd_kernel(*refs, q, t_valid, has_h0):
    refs = list(refs)
    xc_ref, dt_ref, sz_ref = refs.pop(0), refs.pop(0), refs.pop(0)
    h0_ref = refs.pop(0) if has_h0 else None
    dtb_ref, alog_ref, e_ref, dskip_ref, nw_ref, y_ref, hout_ref, ht_ref = refs
    ci = pl.program_id(1)
    rows = xc_ref.shape[1]
    nstate = GROUPS_C * D_STATE_C

    @pl.when(ci == 0)
    def _():
        if has_h0:
            ht_ref[...] = h0_ref[0].reshape(D_INNER_C, D_STATE_C).T
        else:
            ht_ref[...] = jnp.zeros(ht_ref.shape, F32)

    xc = xc_ref[0]
    dtr = dt_ref[0]
    if rows < q:
        xc = jnp.concatenate([xc, jnp.zeros((q - rows, xc.shape[1]), F32)], axis=0)
        dtr = jnp.concatenate([dtr, jnp.zeros((q - rows, dtr.shape[1]), F32)], axis=0)
    x = xc[:, :D_INNER_C]
    bm = xc[:, D_INNER_C:D_INNER_C + nstate]
    cm = xc[:, D_INNER_C + nstate:]

    z = dtr + dtb_ref[...]
    dt = jnp.maximum(z, 0.0) + jnp.log1p(jnp.exp(-jnp.abs(z)))
    row_id = lax.broadcasted_iota(jnp.int32, (q, LANES), 0)
    if t_valid < q:
        dt = jnp.where(row_id < t_valid, dt, 0.0)
    a_neg = -jnp.exp(alog_ref[...])
    r_i = lax.broadcasted_iota(jnp.int32, (q, q), 0)
    c_i = lax.broadcasted_iota(jnp.int32, (q, q), 1)
    causal = r_i >= c_i
    tril = jnp.where(causal, 1.0, 0.0).astype(BF16)
    cum = _dot01(tril, dt * a_neg, left=True)
    cum_last = cum[q - 1:q, :]
    expand = e_ref[...]
    dt_e = _dot01(expand, dt, left=False)
    ecum_e = _dot01(expand, jnp.exp(cum), left=False)
    toend_e = _dot01(expand, jnp.exp(cum_last - cum), left=False)
    cum_t = cum.T
    xdt = x * dt_e
    xdt16 = xdt.astype(BF16)
    xdte16 = (xdt * toend_e).astype(BF16)
    lane = lax.broadcasted_iota(jnp.int32, (q, LANES), 1)

    hg = HEADS_C // GROUPS_C
    gw = hg * HEAD_DIM_C
    ys = []
    for g in range(GROUPS_C):
        bg = bm[:, g * D_STATE_C:(g + 1) * D_STATE_C]
        cg16 = cm[:, g * D_STATE_C:(g + 1) * D_STATE_C].astype(BF16)
        bgt16 = bg.T.astype(BF16)
        cb = jnp.dot(cg16, bgt16, preferred_element_type=F32)
        h_prev = ht_ref[:, g * gw:(g + 1) * gw]
        y_inter = jnp.dot(cg16, h_prev.astype(BF16), preferred_element_type=F32) * ecum_e[:, g * gw:(g + 1) * gw]
        pairs = []
        for k in range(hg // 2):
            xp = xdt16[:, g * gw + k * LANES:g * gw + (k + 1) * LANES]
            halves = []
            for hh in range(2):
                h = g * hg + 2 * k + hh
                seg = cum[:, h:h + 1] - cum_t[h:h + 1, :]
                dec = jnp.exp(jnp.where(causal, seg, NEG_INF))
                halves.append(jnp.dot((cb * dec).astype(BF16), xp, preferred_element_type=F32))
            pairs.append(jnp.where(lane < HEAD_DIM_C, halves[0], halves[1]))
        ys.append(jnp.concatenate(pairs, axis=1) + y_inter)
        ht_ref[:, g * gw:(g + 1) * gw] = (
            h_prev * ecum_e[q - 1:q, g * gw:(g + 1) * gw]
            + jnp.dot(bgt16, xdte16[:, g * gw:(g + 1) * gw], preferred_element_type=F32))
    y = jnp.concatenate(ys, axis=1) + dskip_ref[...] * x
    y = y * sz_ref[0].astype(F32) if rows == q else y[:rows] * sz_ref[0].astype(F32)
    y = y * lax.rsqrt(jnp.mean(y * y, axis=-1, keepdims=True) + EPS) * nw_ref[...]
    y_ref[0] = y.astype(y_ref.dtype)

    @pl.when(ci == pl.num_programs(1) - 1)
    def _():
        hout_ref[0] = ht_ref[...].T.reshape(HEADS_C, HEAD_DIM_C, D_STATE_C)


def _ssd(xc, dt_raw, dt_col, sz, sz_col, h0, dtb, alog, expand, dskip_e, normw):
    bsz, t, _ = xc.shape
    q = SSD_CHUNK
    rows = q if t % q == 0 else t
    assert rows <= q and rows % SUBLANES == 0
    nc = t // rows
    has_h0 = h0 is not None
    in_specs = [pl.BlockSpec((1, rows, CONV_C_DIM), lambda b, c: (b, c, 0)),
                pl.BlockSpec((1, rows, LANES), lambda b, c: (b, c, dt_col)),
                pl.BlockSpec((1, rows, D_INNER_C), lambda b, c: (b, c, sz_col))]
    args = [xc, dt_raw, sz]
    if has_h0:
        in_specs.append(pl.BlockSpec((1, HEADS_C, HEAD_DIM_C, D_STATE_C), lambda b, c: (b, 0, 0, 0)))
        args.append(h0)
    in_specs += [pl.BlockSpec((1, LANES), lambda b, c: (0, 0)), pl.BlockSpec((1, LANES), lambda b, c: (0, 0)),
                 pl.BlockSpec((LANES, D_INNER_C), lambda b, c: (0, 0)),
                 pl.BlockSpec((1, D_INNER_C), lambda b, c: (0, 0)), pl.BlockSpec((1, D_INNER_C), lambda b, c: (0, 0))]
    args += [dtb, alog, expand, dskip_e, normw]
    return pl.pallas_call(
        functools.partial(_ssd_kernel, q=q, t_valid=rows, has_h0=has_h0),
        grid=(bsz, nc),
        in_specs=in_specs,
        out_specs=(pl.BlockSpec((1, rows, D_INNER_C), lambda b, c: (b, c, 0)),
                   pl.BlockSpec((1, HEADS_C, HEAD_DIM_C, D_STATE_C), lambda b, c: (b, 0, 0, 0))),
        out_shape=(jax.ShapeDtypeStruct((bsz, t, D_INNER_C), BF16),
                   jax.ShapeDtypeStruct((bsz, HEADS_C, HEAD_DIM_C, D_STATE_C), F32)),
        scratch_shapes=[pltpu.VMEM((D_STATE_C, D_INNER_C), F32)],
        compiler_params=_params(("parallel", "arbitrary")),
        name="ssd",
    )(*args)


def _lam(lamp_ref, lam_init):
    p = lamp_ref[...]
    s1 = jnp.sum(p[0:1] * p[1:2], axis=-1, keepdims=True)
    s2 = jnp.sum(p[2:3] * p[3:4], axis=-1, keepdims=True)
    return jnp.exp(s1) - jnp.exp(s2) + lam_init


def _diff_finish(o0, o1, lam, subw, lam_init, gate):
    o = o0 - lam * o1
    o = o * lax.rsqrt(jnp.mean(o * o, axis=-1, keepdims=True) + EPS) * subw
    return o * (1.0 - lam_init) * gate


def _online_softmax_step(s, v16, m_ref, l_ref, acc_ref):
    m_prev = m_ref[...]
    m_new = jnp.maximum(m_prev, jnp.max(s, axis=-1, keepdims=True))
    p = jnp.exp2(s - m_new)
    alpha = jnp.exp2(m_prev - m_new)
    l_ref[...] = alpha * l_ref[...] + jnp.sum(p, axis=-1, keepdims=True)
    acc_ref[...] = alpha * acc_ref[...] + jnp.dot(p.astype(BF16), v16, preferred_element_type=F32)
    m_ref[...] = m_new


def _lane_tile(x, width):
    n = width // LANES
    return x if n == 1 else jnp.concatenate([x] * n, axis=1)


def _attn_prompt_kernel(q_ref, k_ref, v_ref, gate_ref, lamp_ref, subw_ref, o_ref,
                        qs_ref, v1_ref, m_ref, acc_ref, *, tq, tk, lam_init):
    qi = pl.program_id(2)
    nblk = 2 * GROUP_B
    nt = (((1,), (1,)), ((), ()))

    @pl.when(qi == 0)
    def _():
        v1_ref[:, :V_DIM_B] = v_ref[0]
        v1_ref[:, V_DIM_B:] = jnp.ones((v1_ref.shape[0], V_DIM_B), BF16)

    lane = lax.broadcasted_iota(jnp.int32, (tq, LANES), 1)
    for g in range(GROUP_B):
        qg = q_ref[0, :, g * LANES:(g + 1) * LANES]
        for c in range(2):
            keep = (lane >= HEAD_DIM_B) if c else (lane < HEAD_DIM_B)
            qs_ref[(2 * g + c) * tq:(2 * g + c + 1) * tq, :] = jnp.where(keep, qg, jnp.zeros_like(qg))
    m_ref[...] = jnp.full(m_ref.shape, NEG_INF, F32)
    acc_ref[...] = jnp.zeros(acc_ref.shape, F32)

    def step(start, width, diagonal):
        kt = k_ref[0, pl.ds(start, width), :]
        vt = v1_ref[pl.ds(start, width), :]
        if diagonal:
            keep = (lax.broadcasted_iota(jnp.int32, (tq, width), 0)
                    >= lax.broadcasted_iota(jnp.int32, (tq, width), 1))
        for i in range(nblk):
            rows = slice(i * tq, (i + 1) * tq)
            s = lax.dot_general(qs_ref[rows, :], kt, nt, preferred_element_type=F32)
            if diagonal:
                s = jnp.where(keep, s, NEG_INF)
            m_prev = m_ref[rows, :]
            m_new = jnp.maximum(m_prev, jnp.max(s, axis=-1, keepdims=True))
            p = jnp.exp2(s - _lane_tile(m_new, width))
            alpha = jnp.exp2(m_prev - m_new)
            pv = jnp.dot(p.astype(BF16), vt, preferred_element_type=F32)
            acc_ref[rows, :] = _lane_tile(alpha, 2 * V_DIM_B) * acc_ref[rows, :] + pv
            m_ref[rows, :] = m_new

    q0 = qi * tq
    n_wide = q0 // tk

    def wide_body(ki, carry):
        step(pl.multiple_of(ki * tk, tk), tk, False)
        return carry

    def narrow_body(j, carry):
        step(pl.multiple_of(n_wide * tk + j * tq, tq), tq, False)
        return carry

    lax.fori_loop(0, n_wide, wide_body, 0)
    if tk > tq:
        lax.fori_loop(0, (q0 - n_wide * tk) // tq, narrow_body, 0)
    step(pl.multiple_of(q0, tq), tq, True)

    lam = _lam(lamp_ref, lam_init)
    for g in range(GROUP_B):
        r0, r1 = (2 * g) * tq, (2 * g + 1) * tq
        o0 = acc_ref[r0:r0 + tq, :V_DIM_B] / acc_ref[r0:r0 + tq, V_DIM_B:]
        o1 = acc_ref[r1:r1 + tq, :V_DIM_B] / acc_ref[r1:r1 + tq, V_DIM_B:]
        gate = gate_ref[0, :, g * LANES:(g + 1) * LANES].astype(F32)
        o_ref[0, :, g * LANES:(g + 1) * LANES] = _diff_finish(
            o0, o1, lam, subw_ref[...], lam_init, gate).astype(o_ref.dtype)


def _attn_prompt(qm, kv16, silu_all, lamp, subw, lam_init, *, tq_pref=256, tk_pref=512):
    bsz, s, _ = kv16.shape
    tq = _tile(s, tq_pref)
    tk = _tile(s, tk_pref)
    assert tk % tq == 0
    hw = GROUP_B * V_DIM_B
    gate_blk0 = BR_WIDTH // hw
    nrow = 2 * GROUP_B * tq
    return pl.pallas_call(
        functools.partial(_attn_prompt_kernel, tq=tq, tk=tk, lam_init=lam_init),
        grid=(bsz, KV_HEADS_B, s // tq),
        in_specs=[pl.BlockSpec((1, tq, hw), lambda b, h, i: (b, i, h)),
                  pl.BlockSpec((1, s, V_DIM_B), lambda b, h, i: (b, 0, h)),
                  pl.BlockSpec((1, s, V_DIM_B), lambda b, h, i: (b, 0, KV_HEADS_B + h)),
                  pl.BlockSpec((1, tq, hw), lambda b, h, i: (b, i, gate_blk0 + h)),
                  pl.BlockSpec((4, HEAD_DIM_B), lambda b, h, i: (0, 0)),
                  pl.BlockSpec((1, V_DIM_B), lambda b, h, i: (0, 0))],
        out_specs=pl.BlockSpec((1, tq, hw), lambda b, h, i: (b, i, h)),
        out_shape=jax.ShapeDtypeStruct((bsz, s, BR_WIDTH), BF16),
        scratch_shapes=[pltpu.VMEM((nrow, LANES), BF16),
                        pltpu.VMEM((s, 2 * V_DIM_B), BF16),
                        pltpu.VMEM((nrow, LANES), F32),
                        pltpu.VMEM((nrow, 2 * V_DIM_B), F32)],
        compiler_params=_params(("parallel", "parallel", "arbitrary")),
        name="attn_prompt",
    )(qm, kv16, kv16, silu_all, lamp, subw)


def _attn_sample_kernel(*refs, pps, t_new, lam_init):
    pt_ref = refs[0]
    del pt_ref
    qb_ref = refs[1]
    k_refs = refs[2:2 + pps]
    v_refs = refs[2 + pps:2 + 2 * pps]
    kn_ref, vn_ref, gate_ref, lamp_ref, subw_ref, o_ref, m_ref, l_ref, acc_ref = refs[2 + 2 * pps:]
    si = pl.program_id(1)
    nrow = qb_ref.shape[1]
    kvw = KV_HEADS_B * V_DIM_B

    @pl.when(si == 0)
    def _():
        m_ref[...] = jnp.full(m_ref.shape, NEG_INF, F32)
        l_ref[...] = jnp.zeros(l_ref.shape, F32)
        acc_ref[...] = jnp.zeros(acc_ref.shape, F32)

    qb = qb_ref[0]
    nt = (((1,), (1,)), ((), ()))
    page = k_refs[0].shape[2] // KV_HEADS_B

    def pages16(page_refs):
        return jnp.concatenate(
            [jnp.concatenate([r[0, 0, pl.ds(h, page, stride=KV_HEADS_B), :] for h in range(KV_HEADS_B)],
                             axis=1).astype(BF16) for r in page_refs],
            axis=0)

    k16 = pages16(k_refs)
    v16 = pages16(v_refs)
    s = lax.dot_general(qb, k16, nt, preferred_element_type=F32)
    _online_softmax_step(s, v16, m_ref, l_ref, acc_ref)

    @pl.when(si == pl.num_programs(1) - 1)
    def _():
        pad = jnp.zeros((LANES - t_new, kvw), BF16)
        kn = jnp.concatenate([kn_ref[0].astype(BF16), pad], axis=0)
        vn = jnp.concatenate([vn_ref[0].astype(BF16), pad], axis=0)
        sn = lax.dot_general(qb, kn, nt, preferred_element_type=F32)
        tok = lax.broadcasted_iota(jnp.int32, (nrow, LANES), 0) % t_new
        col = lax.broadcasted_iota(jnp.int32, (nrow, LANES), 1)
        sn = jnp.where(col <= tok, sn, NEG_INF)
        _online_softmax_step(sn, vn, m_ref, l_ref, acc_ref)
        lam = _lam(lamp_ref, lam_init)
        for h in range(KV_HEADS_B):
            for g in range(GROUP_B):
                r0 = ((h * 2 + 0) * GROUP_B + g) * t_new
                r1 = ((h * 2 + 1) * GROUP_B + g) * t_new
                cs = slice(h * V_DIM_B, (h + 1) * V_DIM_B)
                o0 = acc_ref[r0:r0 + t_new, cs] / l_ref[r0:r0 + t_new, :]
                o1 = acc_ref[r1:r1 + t_new, cs] / l_ref[r1:r1 + t_new, :]
                oc = slice((h * GROUP_B + g) * V_DIM_B, (h * GROUP_B + g + 1) * V_DIM_B)
                gate = gate_ref[0, :, oc].astype(F32)
                o_ref[0, :, oc] = _diff_finish(o0, o1, lam, subw_ref[...], lam_init, gate).astype(o_ref.dtype)


def _attn_sample(qblk, cache_k, cache_v, page_table, layer, kv32, silu_all, lamp, subw, lam_init):
    db, nrow, kvw = qblk.shape
    t_new = kv32.shape[1]
    page_rows = cache_k.shape[2]
    n_pages = page_table.shape[1]
    pps = _tile(n_pages, 8)
    steps = n_pages // pps

    def page_spec(i):
        return pl.BlockSpec((1, 1, page_rows, V_DIM_B), lambda b, s, pt: (pt[b, s * pps + i], layer, 0, 0))

    in_specs = [pl.BlockSpec((1, nrow, kvw), lambda b, s, pt: (b, 0, 0))]
    in_specs += [page_spec(i) for i in range(pps)] + [page_spec(i) for i in range(pps)]
    in_specs += [pl.BlockSpec((1, t_new, kvw), lambda b, s, pt: (b, 0, 0)),
                 pl.BlockSpec((1, t_new, kvw), lambda b, s, pt: (b, 0, 1)),
                 pl.BlockSpec((1, t_new, BR_WIDTH), lambda b, s, pt: (b, 0, 1)),
                 pl.BlockSpec((4, HEAD_DIM_B), lambda b, s, pt: (0, 0)),
                 pl.BlockSpec((1, V_DIM_B), lambda b, s, pt: (0, 0))]
    grid_spec = pltpu.PrefetchScalarGridSpec(
        num_scalar_prefetch=1,
        grid=(db, steps),
        in_specs=in_specs,
        out_specs=pl.BlockSpec((1, t_new, BR_WIDTH), lambda b, s, pt: (b, 0, 0)),
        scratch_shapes=[pltpu.VMEM((nrow, 1), F32), pltpu.VMEM((nrow, 1), F32), pltpu.VMEM((nrow, kvw), F32)])
    return pl.pallas_call(
        functools.partial(_attn_sample_kernel, pps=pps, t_new=t_new, lam_init=lam_init),
        grid_spec=grid_spec,
        out_shape=jax.ShapeDtypeStruct((db, t_new, BR_WIDTH), BF16),
        compiler_params=_params(("parallel", "arbitrary")),
        name="attn_sample",
    )(page_table, qblk, *([cache_k] * pps), *([cache_v] * pps), kv32, kv32, silu_all, lamp, subw)


def _mem_attn_kernel(q_ref, mk_ref, mv_ref, gate_ref, o_ref):
    nt = (((1,), (1,)), ((), ()))
    for h in range(HEADS_M):
        cs = slice(h * HEAD_DIM_M, (h + 1) * HEAD_DIM_M)
        s = lax.dot_general(q_ref[0, :, cs], mk_ref[0, 0, :, cs].astype(BF16), nt, preferred_element_type=F32)
        p = jnp.exp(s - jnp.max(s, axis=-1, keepdims=True))
        p = p / jnp.sum(p, axis=-1, keepdims=True)
        o = jnp.dot(p.astype(BF16), mv_ref[0, 0, :, cs].astype(BF16), preferred_element_type=F32)
        o_ref[0, :, cs] = (o * gate_ref[0, :, cs].astype(F32)).astype(o_ref.dtype)


def _mem_attn(qm, mk, mk_idx, mv, mv_idx, silu_all, *, tq_pref=512):
    bsz, t, _ = qm.shape
    n_mem = mk.shape[2]
    tq = _tile(t, tq_pref)
    return pl.pallas_call(
        _mem_attn_kernel,
        grid=(bsz, t // tq),
        in_specs=[pl.BlockSpec((1, tq, BR_WIDTH), lambda b, i: (b, i, 1)),
                  pl.BlockSpec((1, 1, n_mem, BR_WIDTH), lambda b, i: (b, mk_idx[0], 0, mk_idx[1])),
                  pl.BlockSpec((1, 1, n_mem, BR_WIDTH), lambda b, i: (b, mv_idx[0], 0, mv_idx[1])),
                  pl.BlockSpec((1, tq, BR_WIDTH), lambda b, i: (b, i, 3))],
        out_specs=pl.BlockSpec((1, tq, BR_WIDTH), lambda b, i: (b, i, 0)),
        out_shape=jax.ShapeDtypeStruct((bsz, t, BR_WIDTH), BF16),
        compiler_params=_params(("parallel", "parallel")),
        name="mem_attn",
    )(qm, mk, mv, silu_all)


def _merge_kernel(ya_ref, yb_ref, yc_ref, ym_ref, gates_ref, wb_ref, wo_ref, x_ref, g_ref, b_ref,
                  o32_ref, o16_ref, *, alpha):
    d = x_ref.shape[-1]
    mix = None
    for k, y_ref in enumerate((ya_ref, yb_ref, yc_ref, ym_ref)):
        pk = jnp.dot(y_ref[...], wb_ref[k], preferred_element_type=F32)
        pk = pk * gates_ref[:, k * d:(k + 1) * d].astype(F32)
        mix = pk if mix is None else mix + pk
    out = jnp.dot(mix.astype(BF16), wo_ref[...], preferred_element_type=F32)
    z = alpha * x_ref[...] + out
    mu = jnp.mean(z, axis=-1, keepdims=True)
    zc = z - mu
    var = jnp.mean(zc * zc, axis=-1, keepdims=True)
    y = zc * lax.rsqrt(var + EPS) * g_ref[...] + b_ref[...]
    o32_ref[...] = y
    o16_ref[...] = y.astype(BF16)


def _merge(ya, yb, yc, ym, gates, wb, wo, x, ln_g, ln_b, alpha, *, tm_pref=256):
    m, d = x.shape
    tm = _tile(m, tm_pref)
    row = lambda w: pl.BlockSpec((tm, w), lambda i: (i, 0))
    const = lambda shape: pl.BlockSpec(shape, lambda i: (0,) * len(shape))
    return pl.pallas_call(
        functools.partial(_merge_kernel, alpha=alpha),
        grid=(m // tm,),
        in_specs=[row(BR_WIDTH), row(BR_WIDTH), row(BR_WIDTH), row(BR_WIDTH), row(N_BRANCH * d),
                  const((N_BRANCH, BR_WIDTH, d)), const((d, d)), row(d), const((1, d)), const((1, d))],
        out_specs=(row(d), row(d)),
        out_shape=(jax.ShapeDtypeStruct((m, d), F32), jax.ShapeDtypeStruct((m, d), BF16)),
        compiler_params=_params(("parallel",)),
        name="merge",
    )(ya, yb, yc, ym, gates, wb, wo, x, ln_g, ln_b)


def _layer_weights(l, w_in, b_gate, conv_a_w, conv_a_b, ln_a_g, ln_a_b, lam_q1, lam_k1, lam_q2, lam_k2, subln_w,
                   conv_c_w, conv_c_b, dt_bias, a_log, d_skip, norm_c_w, w_mem_kv, w_branch, w_out, ln_g, ln_b):
    w = w_in[l]
    off, cols = 0, {}
    for name, n in (("a_glu", 2 * BR_WIDTH), ("a_gate", BR_WIDTH), ("b_q", BR_WIDTH), ("b_k", KV_HEADS_B * V_DIM_B),
                    ("b_v", KV_HEADS_B * V_DIM_B), ("b_gate", BR_WIDTH), ("c_z", D_INNER_C), ("c_xbc", CONV_C_DIM),
                    ("c_dt", HEADS_C), ("m_q", BR_WIDTH), ("m_gate", BR_WIDTH), ("g", N_BRANCH * w.shape[0])):
        cols[name] = w[:, off:off + n]
        off += n
    d = w.shape[0]
    pad_h = LANES - HEADS_C
    row = lambda v: v.reshape(1, -1).astype(F32)
    return dict(
        w_glu=cols["a_glu"].astype(BF16),
        w_silu=jnp.concatenate([cols["a_gate"], cols["b_gate"], cols["c_z"], cols["m_gate"]], axis=1).astype(BF16),
        w_qm=jnp.concatenate([cols["b_q"] * (HEAD_DIM_B ** -0.5 * LOG2E), cols["m_q"] * HEAD_DIM_M ** -0.5],
                             axis=1).astype(BF16),
        w_kv=jnp.concatenate([cols["b_k"], cols["b_v"]], axis=1).astype(BF16),
        w_xdt=jnp.concatenate([cols["c_xbc"], cols["c_dt"], jnp.zeros((d, pad_h), F32)], axis=1).astype(BF16),
        w_g=cols["g"].astype(BF16),
        b_gate=row(b_gate[l]),
        conv_a_w=conv_a_w[l], conv_a_b=row(conv_a_b[l]), ln_a_g=row(ln_a_g[l]), ln_a_b=row(ln_a_b[l]),
        lamp=jnp.stack([lam_q1[l], lam_k1[l], lam_q2[l], lam_k2[l]]).astype(F32),
        subln_w=row(subln_w[l]),
        conv_c_w=conv_c_w[l], conv_c_b=row(conv_c_b[l]),
        dtb=row(jnp.pad(dt_bias[l], (0, pad_h))), alog=row(jnp.pad(a_log[l], (0, pad_h))),
        dskip_e=row(jnp.repeat(d_skip[l], HEAD_DIM_C)), norm_c_w=row(norm_c_w[l]),
        w_mem_kv=w_mem_kv[l].astype(BF16),
        w_branch=w_branch[l].astype(BF16), w_out=w_out[l].astype(BF16),
        ln_g=row(ln_g[l]), ln_b=row(ln_b[l]),
        lam_init=0.8 - 0.6 * math.exp(-0.3 * l),
    )


def _layer(x32, x16, lw, attend, mem, conv_a_state, conv_c_state, ssm_state, expand, alpha):
    bsz, t, d = x32.shape
    m = bsz * t
    xf = x16.reshape(m, d)
    u = _proj(xf, lw["w_glu"], epilogue="glu", out_dtype=F32).reshape(bsz, t, BR_WIDTH)
    silu_all = _proj(xf, lw["w_silu"], epilogue="silu", out_dtype=BF16).reshape(bsz, t, 4 * BR_WIDTH)
    qm = _proj(xf, lw["w_qm"], epilogue="none", out_dtype=BF16).reshape(bsz, t, 2 * BR_WIDTH)
    kv32, kv16 = _proj(xf, lw["w_kv"], epilogue="dual")
    xdt = _proj(xf, lw["w_xdt"], epilogue="none", out_dtype=F32, tn_pref=CONV_C_DIM + LANES)
    gates = _proj(xf, lw["w_g"], epilogue="sigmoid_bias", out_dtype=BF16, bias=lw["b_gate"])
    kv32 = kv32.reshape(bsz, t, -1)
    kv16 = kv16.reshape(bsz, t, -1)
    xdt = xdt.reshape(bsz, t, CONV_C_DIM + LANES)

    y_a, new_conv_a = _conv(u, conv_a_state, lw["conv_a_w"], lw["conv_a_b"], post="ln_swish_gate", out_dtype=BF16,
                            ln_g=lw["ln_a_g"], ln_b=lw["ln_a_b"], gate=silu_all, gate_col=0)
    y_b = attend(qm, kv32, kv16, silu_all, lw)
    xc, new_conv_c = _conv(xdt, conv_c_state, lw["conv_c_w"], lw["conv_c_b"], post="silu", out_dtype=F32)
    y_c, new_h = _ssd(xc, xdt, CONV_C_DIM // LANES, silu_all, 2, ssm_state, lw["dtb"], lw["alog"], expand,
                      lw["dskip_e"], lw["norm_c_w"])
    y_m = _mem_attn(qm, mem[0], mem[1], mem[2], mem[3], silu_all)
    flat = lambda a: a.reshape(m, -1)
    x32n, x16n = _merge(flat(y_a), flat(y_b), flat(y_c), flat(y_m), gates, lw["w_branch"], lw["w_out"],
                        x32.reshape(m, d), lw["ln_g"], lw["ln_b"], alpha)
    return x32n.reshape(bsz, t, d), x16n.reshape(bsz, t, d), kv32, new_conv_a, new_conv_c, new_h


def kernel(x_prompt, x_sample, mem_prompt, cache_k, cache_v, page_table, cache_mem_k, cache_mem_v, state_conv_a, state_conv_c, state_ssm, w_in, b_gate, conv_a_w, conv_a_b, ln_a_g, ln_a_b, lam_q1, lam_k1, lam_q2, lam_k2, subln_w, conv_c_w, conv_c_b, dt_bias, a_log, d_skip, norm_c_w, w_mem_kv, w_branch, w_out, ln_g, ln_b):
    bsz, seq, d = x_prompt.shape
    db, t_new, _ = x_sample.shape
    depth = w_in.shape[0]
    n_phys, _, page, _, _ = cache_k.shape
    n_mem = mem_prompt.shape[1]
    alpha = (2 * depth) ** 0.25
    kvw = KV_HEADS_B * V_DIM_B

    ck = cache_k.reshape(n_phys, depth, page * KV_HEADS_B, V_DIM_B)
    cv = cache_v.reshape(n_phys, depth, page * KV_HEADS_B, V_DIM_B)
    cmk = cache_mem_k.reshape(db, depth, n_mem, BR_WIDTH)
    cmv = cache_mem_v.reshape(db, depth, n_mem, BR_WIDTH)
    mem16 = mem_prompt.reshape(bsz * n_mem, d).astype(BF16)
    head_of_lane = jnp.arange(D_INNER_C, dtype=jnp.int32) // HEAD_DIM_C
    expand = (jnp.arange(LANES, dtype=jnp.int32)[:, None] == head_of_lane[None, :]).astype(BF16)
    blk_of_row = jnp.arange(2 * KV_HEADS_B * GROUP_B * t_new, dtype=jnp.int32) // (GROUP_B * t_new)
    blk_of_col = jnp.arange(kvw, dtype=jnp.int32) // HEAD_DIM_B
    qmask = (blk_of_row[:, None] == blk_of_col[None, :])

    xp32, xp16 = x_prompt, x_prompt.astype(BF16)
    xs32, xs16 = x_sample, x_sample.astype(BF16)
    outs = {k: [] for k in ("kp", "vp", "mkp", "mvp", "cap", "ccp", "hp", "ks", "vs", "cas", "ccs", "hs")}
    for l in range(depth):
        lw = _layer_weights(l, w_in, b_gate, conv_a_w, conv_a_b, ln_a_g, ln_a_b, lam_q1, lam_k1, lam_q2, lam_k2,
                            subln_w, conv_c_w, conv_c_b, dt_bias, a_log, d_skip, norm_c_w, w_mem_kv, w_branch,
                            w_out, ln_g, ln_b)
        mkv32 = _proj(mem16, lw["w_mem_kv"], epilogue="none", out_dtype=F32)
        mkv_heads = mkv32.reshape(bsz, 1, n_mem, 2 * BR_WIDTH)

        def attend_prompt(qm, kv32, kv16, silu_all, lw):
            return _attn_prompt(qm, kv16, silu_all, lw["lamp"], lw["subln_w"], lw["lam_init"])

        xp32, xp16, kv32, ca, cc, hh = _layer(xp32, xp16, lw, attend_prompt,
                                               (mkv_heads, (0, 0), mkv_heads, (0, 1)),
                                               None, None, None, expand, alpha)
        outs["kp"].append(kv32[..., :kvw].reshape(bsz, seq // page, page, KV_HEADS_B, V_DIM_B))
        outs["vp"].append(kv32[..., kvw:].reshape(bsz, seq // page, page, KV_HEADS_B, V_DIM_B))
        mkv32 = mkv32.reshape(bsz, n_mem, 2 * BR_WIDTH)
        outs["mkp"].append(mkv32[..., :BR_WIDTH].reshape(bsz, n_mem, HEADS_M, HEAD_DIM_M))
        outs["mvp"].append(mkv32[..., BR_WIDTH:].reshape(bsz, n_mem, HEADS_M, HEAD_DIM_M))
        outs["cap"].append(ca); outs["ccp"].append(cc); outs["hp"].append(hh)

        def attend_sample(qm, kv32, kv16, silu_all, lw, l=l):
            q = qm[..., :BR_WIDTH].reshape(db, t_new, KV_HEADS_B, GROUP_B, 2, HEAD_DIM_B)
            q = q.transpose(0, 2, 4, 3, 1, 5).reshape(db, 2 * KV_HEADS_B * GROUP_B * t_new, 1, HEAD_DIM_B)
            qblk = jnp.where(qmask.reshape(1, -1, 2 * KV_HEADS_B, HEAD_DIM_B), q, jnp.zeros((), BF16))
            qblk = qblk.reshape(db, -1, kvw)
            return _attn_sample(qblk, ck, cv, page_table, l, kv32, silu_all, lw["lamp"], lw["subln_w"],
                                lw["lam_init"])

        xs32, xs16, kv32, ca, cc, hh = _layer(xs32, xs16, lw, attend_sample, (cmk, (l, 0), cmv, (l, 0)),
                                               state_conv_a[:, l], state_conv_c[:, l], state_ssm[:, l],
                                               expand, alpha)
        outs["ks"].append(kv32[..., :kvw].reshape(db, t_new, KV_HEADS_B, V_DIM_B))
        outs["vs"].append(kv32[..., kvw:].reshape(db, t_new, KV_HEADS_B, V_DIM_B))
        outs["cas"].append(ca); outs["ccs"].append(cc); outs["hs"].append(hh)

    st = lambda k, axis: jnp.stack(outs[k], axis=axis)
    return (xp32, xs32, st("kp", 2), st("vp", 2), st("mkp", 1), st("mvp", 1), st("cap", 1), st("ccp", 1),
            st("hp", 1), st("ks", 1), st("vs", 1), st("cas", 1), st("ccs", 1), st("hs", 1))
```

```python
import functools
import math

import jax
import jax.numpy as jnp
from jax import lax
from jax.experimental import pallas as pl
from jax.experimental.pallas import tpu as pltpu

F32 = jnp.float32
BF16 = jnp.bfloat16

LANES = 128
SUBLANES = 8
VMEM_BYTES_V7X = 64 * 1024 * 1024
VMEM_LIMIT = (VMEM_BYTES_V7X * 3) // 4

BR_WIDTH = 1024
N_BRANCH = 4
CONV_A_WIDTH = 31
HEADS_B = 8
KV_HEADS_B = 4
GROUP_B = HEADS_B // KV_HEADS_B
HEAD_DIM_B = 64
V_DIM_B = 2 * HEAD_DIM_B
D_INNER_C = 1024
HEAD_DIM_C = 64
HEADS_C = D_INNER_C // HEAD_DIM_C
GROUPS_C = 4
D_STATE_C = 128
CONV_C_WIDTH = 4
CONV_C_DIM = D_INNER_C + 2 * GROUPS_C * D_STATE_C
SSD_CHUNK = 128
HEADS_M = 4
HEAD_DIM_M = BR_WIDTH // HEADS_M
EPS = 1e-5
NEG_INF = float("-inf")
LOG2E = math.log2(math.e)


def _tile(n, pref):
    if n <= pref:
        return n
    t = pref
    while n % t:
        t //= 2
    return t


def _sigmoid(x):
    return 1.0 / (1.0 + jnp.exp(-x))


def _silu(x):
    return x * _sigmoid(x)


def _params(sem):
    return pltpu.CompilerParams(dimension_semantics=sem, vmem_limit_bytes=VMEM_LIMIT)


def _proj_kernel(*refs, epilogue):
    x = refs[0][...]
    if epilogue == "glu":
        _, w1_ref, w2_ref, o_ref = refs
        a1 = jnp.dot(x, w1_ref[...], preferred_element_type=F32)
        a2 = jnp.dot(x, w2_ref[...], preferred_element_type=F32)
        o_ref[...] = (a1 * _sigmoid(a2)).astype(o_ref.dtype)
    elif epilogue == "sigmoid_bias":
        _, w_ref, b_ref, o_ref = refs
        y = jnp.dot(x, w_ref[...], preferred_element_type=F32)
        o_ref[...] = _sigmoid(y + b_ref[...]).astype(o_ref.dtype)
    elif epilogue == "dual":
        _, w_ref, o32_ref, o16_ref = refs
        y = jnp.dot(x, w_ref[...], preferred_element_type=F32)
        o32_ref[...] = y
        o16_ref[...] = y.astype(BF16)
    elif epilogue == "kv_pages":
        _, w_ref, _, _, ko_ref, vo_ref, o16_ref = refs
        y = jnp.dot(x, w_ref[...], preferred_element_type=F32)
        o16_ref[...] = y.astype(BF16)
        page = ko_ref.shape[3] // KV_HEADS_B
        kvw = KV_HEADS_B * V_DIM_B
        for j in range(ko_ref.shape[1]):
            for h in range(KV_HEADS_B):
                rows = pl.ds(h, page, stride=KV_HEADS_B)
                ko_ref[0, j, 0, rows, :] = y[j * page:(j + 1) * page, h * V_DIM_B:(h + 1) * V_DIM_B]
                vo_ref[0, j, 0, rows, :] = y[j * page:(j + 1) * page, kvw + h * V_DIM_B:kvw + (h + 1) * V_DIM_B]
    else:
        _, w_ref, o_ref = refs
        y = jnp.dot(x, w_ref[...], preferred_element_type=F32)
        if epilogue == "silu":
            y = _silu(y)
        o_ref[...] = y.astype(o_ref.dtype)


def _proj(x, w, *, epilogue, out_dtype=F32, bias=None, tn_pref=1024, tm_pref=1024):
    m, k = x.shape
    n = w.shape[1] // 2 if epilogue == "glu" else w.shape[1]
    tm, tn = _tile(m, tm_pref), _tile(n, tn_pref)
    nj = n // tn
    x_spec = pl.BlockSpec((tm, k), lambda j, i: (i, 0))
    w_spec = pl.BlockSpec((k, tn), lambda j, i: (0, j))
    o_spec = pl.BlockSpec((tm, tn), lambda j, i: (i, j))
    in_specs, args = [x_spec, w_spec], [x, w]
    out_shape, out_specs = jax.ShapeDtypeStruct((m, n), out_dtype), o_spec
    if epilogue == "glu":
        in_specs.append(pl.BlockSpec((k, tn), lambda j, i: (0, j + nj)))
        args.append(w)
    elif epilogue == "sigmoid_bias":
        in_specs.append(pl.BlockSpec((1, tn), lambda j, i: (0, j)))
        args.append(bias)
    elif epilogue == "dual":
        out_shape = (jax.ShapeDtypeStruct((m, n), F32), jax.ShapeDtypeStruct((m, n), BF16))
        out_specs = (o_spec, o_spec)
    return pl.pallas_call(
        functools.partial(_proj_kernel, epilogue=epilogue),
        grid=(nj, m // tm),
        in_specs=in_specs,
        out_specs=out_specs,
        out_shape=out_shape,
        compiler_params=_params(("parallel", "parallel")),
        name="proj_" + epilogue,
    )(*args)


def _proj_kv_pages(x, w, k_pages, v_pages, layer, seq, *, tm_pref=1024):
    m, k = x.shape
    n = w.shape[1]
    page = k_pages.shape[3] // KV_HEADS_B
    tm = _tile(seq, tm_pref)
    per_seq = seq // tm
    pg_spec = pl.BlockSpec((1, tm // page, 1, page * KV_HEADS_B, V_DIM_B),
                           lambda j, i: (i // per_seq, i % per_seq, layer, 0, 0))
    return pl.pallas_call(
        functools.partial(_proj_kernel, epilogue="kv_pages"),
        grid=(1, m // tm),
        in_specs=[pl.BlockSpec((tm, k), lambda j, i: (i, 0)), pl.BlockSpec((k, n), lambda j, i: (0, 0)),
                  pl.BlockSpec(memory_space=pl.ANY), pl.BlockSpec(memory_space=pl.ANY)],
        out_specs=(pg_spec, pg_spec, pl.BlockSpec((tm, n), lambda j, i: (i, 0))),
        out_shape=(jax.ShapeDtypeStruct(k_pages.shape, F32), jax.ShapeDtypeStruct(v_pages.shape, F32),
                   jax.ShapeDtypeStruct((m, n), BF16)),
        input_output_aliases={2: 0, 3: 1},
        compiler_params=_params(("parallel", "parallel")),
        name="proj_kv_pages",
    )(x, w, k_pages, v_pages)


def _conv_kernel(*refs, taps, hist, tt, rb, post, has_state):
    refs = list(refs)
    x_ref = refs.pop(0)
    state_ref = refs.pop(0) if has_state else None
    w_ref, b_ref = refs.pop(0), refs.pop(0)
    if post == "ln_swish_gate":
        g_ref, beta_ref, gate_ref = refs.pop(0), refs.pop(0), refs.pop(0)
    y_ref, ns_ref, win_ref, sh_ref = refs
    t = pl.program_id(1)
    back = taps - 1
    c = x_ref.shape[-1]
    n_win = hist + tt

    @pl.when(t == 0)
    def _():
        win_ref[0:hist, :] = jnp.zeros((hist, c), F32)
        win_ref[n_win:n_win + SUBLANES, :] = jnp.zeros((SUBLANES, c), F32)
        if has_state:
            win_ref[hist - back:hist, :] = state_ref[0].astype(F32)

    win_ref[hist:n_win, :] = x_ref[0].astype(F32)

    offsets = [j + hist - back for j in range(taps)]
    for p in sorted({o % SUBLANES for o in offsets} - {0}):
        sh_ref[p - 1, :, :] = win_ref[p:p + n_win, :]

    for r0 in range(0, tt, rb):
        acc = jnp.broadcast_to(b_ref[...], (rb, c))
        for j, o in enumerate(offsets):
            a, p = divmod(o, SUBLANES)
            lo = r0 + a * SUBLANES
            xs = win_ref[lo:lo + rb, :] if p == 0 else sh_ref[p - 1, lo:lo + rb, :]
            wj = w_ref[j * SUBLANES:(j + 1) * SUBLANES, :]
            acc = acc + (wj if rb == SUBLANES else jnp.concatenate([wj] * (rb // SUBLANES), axis=0)) * xs
        if post == "ln_swish_gate":
            mu = jnp.mean(acc, axis=-1, keepdims=True)
            d = acc - mu
            var = jnp.mean(d * d, axis=-1, keepdims=True)
            y = d * lax.rsqrt(var + EPS) * g_ref[...] + beta_ref[...]
            y = _silu(y) * gate_ref[0, r0:r0 + rb, :].astype(F32)
        else:
            y = _silu(acc)
        y_ref[0, r0:r0 + rb, :] = y.astype(y_ref.dtype)

    @pl.when(t == pl.num_programs(1) - 1)
    def _():
        ns_ref[0] = win_ref[n_win - back:n_win, :]

    win_ref[0:hist, :] = win_ref[tt:tt + hist, :]


def _conv(x, state, w, b, *, post, out_dtype, ln_g=None, ln_b=None, gate=None, gate_col=0, tt_pref=256):
    bsz, t, _ = x.shape
    taps, c = w.shape
    back = taps - 1
    hist = -(-back // SUBLANES) * SUBLANES
    tt = _tile(t, tt_pref)
    rb = min(tt, max(SUBLANES, (32 * 1024) // c))
    has_state = state is not None
    in_specs = [pl.BlockSpec((1, tt, c), lambda i, j: (i, j, 0))]
    args = [x]
    if has_state:
        in_specs.append(pl.BlockSpec((1, back, c), lambda i, j: (i, 0, 0)))
        args.append(state)
    in_specs += [pl.BlockSpec((taps * SUBLANES, c), lambda i, j: (0, 0)), pl.BlockSpec((1, c), lambda i, j: (0, 0))]
    args += [jnp.repeat(w, SUBLANES, axis=0), b]
    if post == "ln_swish_gate":
        in_specs += [pl.BlockSpec((1, c), lambda i, j: (0, 0)), pl.BlockSpec((1, c), lambda i, j: (0, 0)),
                     pl.BlockSpec((1, tt, c), lambda i, j: (i, j, gate_col))]
        args += [ln_g, ln_b, gate]
    return pl.pallas_call(
        functools.partial(_conv_kernel, taps=taps, hist=hist, tt=tt, rb=rb, post=post, has_state=has_state),
        grid=(bsz, t // tt),
        in_specs=in_specs,
        out_specs=(pl.BlockSpec((1, tt, c), lambda i, j: (i, j, 0)),
                   pl.BlockSpec((1, back, c), lambda i, j: (i, 0, 0))),
        out_shape=(jax.ShapeDtypeStruct((bsz, t, c), out_dtype), jax.ShapeDtypeStruct((bsz, back, c), F32)),
        scratch_shapes=[pltpu.VMEM((hist + tt + SUBLANES, c), F32),
                        pltpu.VMEM((SUBLANES - 1, hist + tt, c), F32)],
        compiler_params=_params(("parallel", "arbitrary")),
        name="conv_" + post,
    )(*args)


def _split3(x):
    hi = x.astype(BF16)
    r1 = x - hi.astype(F32)
    mid = r1.astype(BF16)
    lo = (r1 - mid.astype(F32)).astype(BF16)
    return hi, mid, lo


def _dot01(a01, x, *, left):
    out = None
    for piece in _split3(x):
        y = (jnp.dot(a01, piece, preferred_element_type=F32) if left
             else jnp.dot(piece, a01, preferred_element_type=F32))
        out = y if out is None else out + y
    return out


def _ssd_kernel(*refs, q, t_valid, has_h0):
    refs = list(refs)
    xc_ref, dt_ref, sz_ref = refs.pop(0), refs.pop(0), refs.pop(0)
    h0_ref = refs.pop(0) if has_h0 else None
    dtb_ref, alog_ref, e_ref, dskip_ref, nw_ref, y_ref, hout_ref, ht_ref = refs
    ci = pl.program_id(1)
    rows = xc_ref.shape[1]
    nstate = GROUPS_C * D_STATE_C

    @pl.when(ci == 0)
    def _():
        if has_h0:
            ht_ref[...] = h0_ref[0].reshape(D_INNER_C, D_STATE_C).T
        else:
            ht_ref[...] = jnp.zeros(ht_ref.shape, F32)

    xc = xc_ref[0]
    dtr = dt_ref[0]
    if rows < q:
        xc = jnp.concatenate([xc, jnp.zeros((q - rows, xc.shape[1]), F32)], axis=0)
        dtr = jnp.concatenate([dtr, jnp.zeros((q - rows, dtr.shape[1]), F32)], axis=0)
    x = xc[:, :D_INNER_C]
    bm = xc[:, D_INNER_C:D_INNER_C + nstate]
    cm = xc[:, D_INNER_C + nstate:]

    z = dtr + dtb_ref[...]
    dt = jnp.maximum(z, 0.0) + jnp.log1p(jnp.exp(-jnp.abs(z)))
    row_id = lax.broadcasted_iota(jnp.int32, (q, LANES), 0)
    if t_valid < q:
        dt = jnp.where(row_id < t_valid, dt, 0.0)
    a_neg = -jnp.exp(alog_ref[...])
    r_i = lax.broadcasted_iota(jnp.int32, (q, q), 0)
    c_i = lax.broadcasted_iota(jnp.int32, (q, q), 1)
    causal = r_i >= c_i
    tril = jnp.where(causal, 1.0, 0.0).astype(BF16)
    cum = _dot01(tril, dt * a_neg, left=True)
    cum_last = cum[q - 1:q, :]
    expand = e_ref[...]
    dt_e = _dot01(expand, dt, left=False)
    ecum_e = _dot01(expand, jnp.exp(cum), left=False)
    toend_e = _dot01(expand, jnp.exp(cum_last - cum), left=False)
    cum_t = cum.T
    xdt = x * dt_e
    xdt16 = xdt.astype(BF16)
    xdte16 = (xdt * toend_e).astype(BF16)
    lane = lax.broadcasted_iota(jnp.int32, (q, LANES), 1)

    hg = HEADS_C // GROUPS_C
    gw = hg * HEAD_DIM_C
    ys = []
    for g in range(GROUPS_C):
        bg = bm[:, g * D_STATE_C:(g + 1) * D_STATE_C]
        cg16 = cm[:, g * D_STATE_C:(g + 1) * D_STATE_C].astype(BF16)
        bgt16 = bg.T.astype(BF16)
        cb = jnp.dot(cg16, bgt16, preferred_element_type=F32)
        h_prev = ht_ref[:, g * gw:(g + 1) * gw]
        y_inter = jnp.dot(cg16, h_prev.astype(BF16), preferred_element_type=F32) * ecum_e[:, g * gw:(g + 1) * gw]
        pairs = []
        for k in range(hg // 2):
            xp = xdt16[:, g * gw + k * LANES:g * gw + (k + 1) * LANES]
            halves = []
            for hh in range(2):
                h = g * hg + 2 * k + hh
                seg = cum[:, h:h + 1] - cum_t[h:h + 1, :]
                dec = jnp.exp(jnp.where(causal, seg, NEG_INF))
                halves.append(jnp.dot((cb * dec).astype(BF16), xp, preferred_element_type=F32))
            pairs.append(jnp.where(lane < HEAD_DIM_C, halves[0], halves[1]))
        ys.append(jnp.concatenate(pairs, axis=1) + y_inter)
        ht_ref[:, g * gw:(g + 1) * gw] = (
            h_prev * ecum_e[q - 1:q, g * gw:(g + 1) * gw]
            + jnp.dot(bgt16, xdte16[:, g * gw:(g + 1) * gw], preferred_element_type=F32))
    y = jnp.concatenate(ys, axis=1) + dskip_ref[...] * x
    y = y * sz_ref[0].astype(F32) if rows == q else y[:rows] * sz_ref[0].astype(F32)
    y = y * lax.rsqrt(jnp.mean(y * y, axis=-1, keepdims=True) + EPS) * nw_ref[...]
    y_ref[0] = y.astype(y_ref.dtype)

    @pl.when(ci == pl.num_programs(1) - 1)
    def _():
        hout_ref[0] = ht_ref[...].T.reshape(HEADS_C, HEAD_DIM_C, D_STATE_C)


def _ssd(xc, dt_raw, dt_col, sz, sz_col, h0, dtb, alog, expand, dskip_e, normw):
    bsz, t, _ = xc.shape
    q = SSD_CHUNK
    rows = q if t % q == 0 else t
    assert rows <= q and rows % SUBLANES == 0
    nc = t // rows
    has_h0 = h0 is not None
    in_specs = [pl.BlockSpec((1, rows, CONV_C_DIM), lambda b, c: (b, c, 0)),
                pl.BlockSpec((1, rows, LANES), lambda b, c: (b, c, dt_col)),
                pl.BlockSpec((1, rows, D_INNER_C), lambda b, c: (b, c, sz_col))]
    args = [xc, dt_raw, sz]
    if has_h0:
        in_specs.append(pl.BlockSpec((1, HEADS_C, HEAD_DIM_C, D_STATE_C), lambda b, c: (b, 0, 0, 0)))
        args.append(h0)
    in_specs += [pl.BlockSpec((1, LANES), lambda b, c: (0, 0)), pl.BlockSpec((1, LANES), lambda b, c: (0, 0)),
                 pl.BlockSpec((LANES, D_INNER_C), lambda b, c: (0, 0)),
                 pl.BlockSpec((1, D_INNER_C), lambda b, c: (0, 0)), pl.BlockSpec((1, D_INNER_C), lambda b, c: (0, 0))]
    args += [dtb, alog, expand, dskip_e, normw]
    return pl.pallas_call(
        functools.partial(_ssd_kernel, q=q, t_valid=rows, has_h0=has_h0),
        grid=(bsz, nc),
        in_specs=in_specs,
        out_specs=(pl.BlockSpec((1, rows, D_INNER_C), lambda b, c: (b, c, 0)),
                   pl.BlockSpec((1, HEADS_C, HEAD_DIM_C, D_STATE_C), lambda b, c: (b, 0, 0, 0))),
        out_shape=(jax.ShapeDtypeStruct((bsz, t, D_INNER_C), BF16),
                   jax.ShapeDtypeStruct((bsz, HEADS_C, HEAD_DIM_C, D_STATE_C), F32)),
        scratch_shapes=[pltpu.VMEM((D_STATE_C, D_INNER_C), F32)],
        compiler_params=_params(("parallel", "arbitrary")),
        name="ssd",
    )(*args)


def _lam(lamp_ref, lam_init):
    p = lamp_ref[...]
    s1 = jnp.sum(p[0:1] * p[1:2], axis=-1, keepdims=True)
    s2 = jnp.sum(p[2:3] * p[3:4], axis=-1, keepdims=True)
    return jnp.exp(s1) - jnp.exp(s2) + lam_init


def _diff_finish(o0, o1, lam, subw, lam_init, gate):
    o = o0 - lam * o1
    o = o * lax.rsqrt(jnp.mean(o * o, axis=-1, keepdims=True) + EPS) * subw
    return o * (1.0 - lam_init) * gate


def _online_softmax_step(s, v16, m_ref, l_ref, acc_ref):
    m_prev = m_ref[...]
    m_new = jnp.maximum(m_prev, jnp.max(s, axis=-1, keepdims=True))
    p = jnp.exp2(s - m_new)
    alpha = jnp.exp2(m_prev - m_new)
    l_ref[...] = alpha * l_ref[...] + jnp.sum(p, axis=-1, keepdims=True)
    acc_ref[...] = alpha * acc_ref[...] + jnp.dot(p.astype(BF16), v16, preferred_element_type=F32)
    m_ref[...] = m_new


def _lane_tile(x, width):
    n = width // LANES
    return x if n == 1 else jnp.concatenate([x] * n, axis=1)


def _attn_prompt_kernel(q_ref, k_ref, v_ref, gate_ref, lamp_ref, subw_ref, o_ref,
                        qs_ref, v1_ref, m_ref, acc_ref, *, tq, tk, lam_init):
    qi = pl.program_id(2)
    nblk = 2 * GROUP_B
    nt = (((1,), (1,)), ((), ()))

    @pl.when(qi == 0)
    def _():
        v1_ref[:, :V_DIM_B] = v_ref[0]
        v1_ref[:, V_DIM_B:] = jnp.ones((v1_ref.shape[0], V_DIM_B), BF16)

    lane = lax.broadcasted_iota(jnp.int32, (tq, LANES), 1)
    for g in range(GROUP_B):
        qg = q_ref[0, :, g * LANES:(g + 1) * LANES]
        for c in range(2):
            keep = (lane >= HEAD_DIM_B) if c else (lane < HEAD_DIM_B)
            qs_ref[(2 * g + c) * tq:(2 * g + c + 1) * tq, :] = jnp.where(keep, qg, jnp.zeros_like(qg))
    m_ref[...] = jnp.full(m_ref.shape, NEG_INF, F32)
    acc_ref[...] = jnp.zeros(acc_ref.shape, F32)

    def step(start, width, diagonal):
        kt = k_ref[0, pl.ds(start, width), :]
        vt = v1_ref[pl.ds(start, width), :]
        if diagonal:
            keep = (lax.broadcasted_iota(jnp.int32, (tq, width), 0)
                    >= lax.broadcasted_iota(jnp.int32, (tq, width), 1))
        for i in range(nblk):
            rows = slice(i * tq, (i + 1) * tq)
            s = lax.dot_general(qs_ref[rows, :], kt, nt, preferred_element_type=F32)
            if diagonal:
                s = jnp.where(keep, s, NEG_INF)
            m_prev = m_ref[rows, :]
            m_new = jnp.maximum(m_prev, jnp.max(s, axis=-1, keepdims=True))
            p = jnp.exp2(s - _lane_tile(m_new, width))
            alpha = jnp.exp2(m_prev - m_new)
            pv = jnp.dot(p.astype(BF16), vt, preferred_element_type=F32)
            acc_ref[rows, :] = _lane_tile(alpha, 2 * V_DIM_B) * acc_ref[rows, :] + pv
            m_ref[rows, :] = m_new

    q0 = qi * tq
    n_wide = q0 // tk

    def wide_body(ki, carry):
        step(pl.multiple_of(ki * tk, tk), tk, False)
        return carry

    def narrow_body(j, carry):
        step(pl.multiple_of(n_wide * tk + j * tq, tq), tq, False)
        return carry

    lax.fori_loop(0, n_wide, wide_body, 0)
    if tk > tq:
        lax.fori_loop(0, (q0 - n_wide * tk) // tq, narrow_body, 0)
    step(pl.multiple_of(q0, tq), tq, True)

    lam = _lam(lamp_ref, lam_init)
    for g in range(GROUP_B):
        r0, r1 = (2 * g) * tq, (2 * g + 1) * tq
        o0 = acc_ref[r0:r0 + tq, :V_DIM_B] / acc_ref[r0:r0 + tq, V_DIM_B:]
        o1 = acc_ref[r1:r1 + tq, :V_DIM_B] / acc_ref[r1:r1 + tq, V_DIM_B:]
        gate = gate_ref[0, :, g * LANES:(g + 1) * LANES].astype(F32)
        o_ref[0, :, g * LANES:(g + 1) * LANES] = _diff_finish(
            o0, o1, lam, subw_ref[...], lam_init, gate).astype(o_ref.dtype)


def _attn_prompt(qm, kv16, silu_all, lamp, subw, lam_init, *, tq_pref=256, tk_pref=512):
    bsz, s, _ = kv16.shape
    tq = _tile(s, tq_pref)
    tk = _tile(s, tk_pref)
    assert tk % tq == 0
    hw = GROUP_B * V_DIM_B
    gate_blk0 = BR_WIDTH // hw
    nrow = 2 * GROUP_B * tq
    return pl.pallas_call(
        functools.partial(_attn_prompt_kernel, tq=tq, tk=tk, lam_init=lam_init),
        grid=(bsz, KV_HEADS_B, s // tq),
        in_specs=[pl.BlockSpec((1, tq, hw), lambda b, h, i: (b, i, h)),
                  pl.BlockSpec((1, s, V_DIM_B), lambda b, h, i: (b, 0, h)),
                  pl.BlockSpec((1, s, V_DIM_B), lambda b, h, i: (b, 0, KV_HEADS_B + h)),
                  pl.BlockSpec((1, tq, hw), lambda b, h, i: (b, i, gate_blk0 + h)),
                  pl.BlockSpec((4, HEAD_DIM_B), lambda b, h, i: (0, 0)),
                  pl.BlockSpec((1, V_DIM_B), lambda b, h, i: (0, 0))],
        out_specs=pl.BlockSpec((1, tq, hw), lambda b, h, i: (b, i, h)),
        out_shape=jax.ShapeDtypeStruct((bsz, s, BR_WIDTH), BF16),
        scratch_shapes=[pltpu.VMEM((nrow, LANES), BF16),
                        pltpu.VMEM((s, 2 * V_DIM_B), BF16),
                        pltpu.VMEM((nrow, LANES), F32),
                        pltpu.VMEM((nrow, 2 * V_DIM_B), F32)],
        compiler_params=_params(("parallel", "parallel", "arbitrary")),
        name="attn_prompt",
    )(qm, kv16, kv16, silu_all, lamp, subw)


def _attn_sample_kernel(*refs, pps, t_new, lam_init):
    pt_ref = refs[0]
    del pt_ref
    qb_ref = refs[1]
    k_refs = refs[2:2 + pps]
    v_refs = refs[2 + pps:2 + 2 * pps]
    kn_ref, vn_ref, gate_ref, lamp_ref, subw_ref, o_ref, m_ref, l_ref, acc_ref = refs[2 + 2 * pps:]
    si = pl.program_id(1)
    nrow = qb_ref.shape[1]
    kvw = KV_HEADS_B * V_DIM_B

    @pl.when(si == 0)
    def _():
        m_ref[...] = jnp.full(m_ref.shape, NEG_INF, F32)
        l_ref[...] = jnp.zeros(l_ref.shape, F32)
        acc_ref[...] = jnp.zeros(acc_ref.shape, F32)

    qb = qb_ref[0]
    nt = (((1,), (1,)), ((), ()))
    page = k_refs[0].shape[2] // KV_HEADS_B

    def pages16(page_refs):
        return jnp.concatenate(
            [jnp.concatenate([r[0, 0, pl.ds(h, page, stride=KV_HEADS_B), :] for h in range(KV_HEADS_B)],
                             axis=1).astype(BF16) for r in page_refs],
            axis=0)

    k16 = pages16(k_refs)
    v16 = pages16(v_refs)
    s = lax.dot_general(qb, k16, nt, preferred_element_type=F32)
    _online_softmax_step(s, v16, m_ref, l_ref, acc_ref)

    @pl.when(si == pl.num_programs(1) - 1)
    def _():
        pad = jnp.zeros((LANES - t_new, kvw), BF16)
        kn = jnp.concatenate([kn_ref[0].astype(BF16), pad], axis=0)
        vn = jnp.concatenate([vn_ref[0].astype(BF16), pad], axis=0)
        sn = lax.dot_general(qb, kn, nt, preferred_element_type=F32)
        tok = lax.broadcasted_iota(jnp.int32, (nrow, LANES), 0) % t_new
        col = lax.broadcasted_iota(jnp.int32, (nrow, LANES), 1)
        sn = jnp.where(col <= tok, sn, NEG_INF)
        _online_softmax_step(sn, vn, m_ref, l_ref, acc_ref)
        lam = _lam(lamp_ref, lam_init)
        for h in range(KV_HEADS_B):
            for g in range(GROUP_B):
                r0 = ((h * 2 + 0) * GROUP_B + g) * t_new
                r1 = ((h * 2 + 1) * GROUP_B + g) * t_new
                cs = slice(h * V_DIM_B, (h + 1) * V_DIM_B)
                o0 = acc_ref[r0:r0 + t_new, cs] / l_ref[r0:r0 + t_new, :]
                o1 = acc_ref[r1:r1 + t_new, cs] / l_ref[r1:r1 + t_new, :]
                oc = slice((h * GROUP_B + g) * V_DIM_B, (h * GROUP_B + g + 1) * V_DIM_B)
                gate = gate_ref[0, :, oc].astype(F32)
                o_ref[0, :, oc] = _diff_finish(o0, o1, lam, subw_ref[...], lam_init, gate).astype(o_ref.dtype)


def _attn_sample(qblk, cache_k, cache_v, page_table, layer, kv32, silu_all, lamp, subw, lam_init):
    db, nrow, kvw = qblk.shape
    t_new = kv32.shape[1]
    page_rows = cache_k.shape[2]
    n_pages = page_table.shape[1]
    pps = _tile(n_pages, 8)
    steps = n_pages // pps

    def page_spec(i):
        return pl.BlockSpec((1, 1, page_rows, V_DIM_B), lambda b, s, pt: (pt[b, s * pps + i], layer, 0, 0))

    in_specs = [pl.BlockSpec((1, nrow, kvw), lambda b, s, pt: (b, 0, 0))]
    in_specs += [page_spec(i) for i in range(pps)] + [page_spec(i) for i in range(pps)]
    in_specs += [pl.BlockSpec((1, t_new, kvw), lambda b, s, pt: (b, 0, 0)),
                 pl.BlockSpec((1, t_new, kvw), lambda b, s, pt: (b, 0, 1)),
                 pl.BlockSpec((1, t_new, BR_WIDTH), lambda b, s, pt: (b, 0, 1)),
                 pl.BlockSpec((4, HEAD_DIM_B), lambda b, s, pt: (0, 0)),
                 pl.BlockSpec((1, V_DIM_B), lambda b, s, pt: (0, 0))]
    grid_spec = pltpu.PrefetchScalarGridSpec(
        num_scalar_prefetch=1,
        grid=(db, steps),
        in_specs=in_specs,
        out_specs=pl.BlockSpec((1, t_new, BR_WIDTH), lambda b, s, pt: (b, 0, 0)),
        scratch_shapes=[pltpu.VMEM((nrow, 1), F32), pltpu.VMEM((nrow, 1), F32), pltpu.VMEM((nrow, kvw), F32)])
    return pl.pallas_call(
        functools.partial(_attn_sample_kernel, pps=pps, t_new=t_new, lam_init=lam_init),
        grid_spec=grid_spec,
        out_shape=jax.ShapeDtypeStruct((db, t_new, BR_WIDTH), BF16),
        compiler_params=_params(("parallel", "arbitrary")),
        name="attn_sample",
    )(page_table, qblk, *([cache_k] * pps), *([cache_v] * pps), kv32, kv32, silu_all, lamp, subw)


def _mem_attn_kernel(q_ref, mk_ref, mv_ref, gate_ref, o_ref):
    nt = (((1,), (1,)), ((), ()))
    for h in range(HEADS_M):
        cs = slice(h * HEAD_DIM_M, (h + 1) * HEAD_DIM_M)
        s = lax.dot_general(q_ref[0, :, cs], mk_ref[0, 0, :, cs].astype(BF16), nt, preferred_element_type=F32)
        p = jnp.exp(s - jnp.max(s, axis=-1, keepdims=True))
        p = p / jnp.sum(p, axis=-1, keepdims=True)
        o = jnp.dot(p.astype(BF16), mv_ref[0, 0, :, cs].astype(BF16), preferred_element_type=F32)
        o_ref[0, :, cs] = (o * gate_ref[0, :, cs].astype(F32)).astype(o_ref.dtype)


def _mem_attn(qm, mk, mk_idx, mv, mv_idx, silu_all, *, tq_pref=512):
    bsz, t, _ = qm.shape
    n_mem = mk.shape[2]
    tq = _tile(t, tq_pref)
    return pl.pallas_call(
        _mem_attn_kernel,
        grid=(bsz, t // tq),
        in_specs=[pl.BlockSpec((1, tq, BR_WIDTH), lambda b, i: (b, i, 1)),
                  pl.BlockSpec((1, 1, n_mem, BR_WIDTH), lambda b, i: (b, mk_idx[0], 0, mk_idx[1])),
                  pl.BlockSpec((1, 1, n_mem, BR_WIDTH), lambda b, i: (b, mv_idx[0], 0, mv_idx[1])),
                  pl.BlockSpec((1, tq, BR_WIDTH), lambda b, i: (b, i, 3))],
        out_specs=pl.BlockSpec((1, tq, BR_WIDTH), lambda b, i: (b, i, 0)),
        out_shape=jax.ShapeDtypeStruct((bsz, t, BR_WIDTH), BF16),
        compiler_params=_params(("parallel", "parallel")),
        name="mem_attn",
    )(qm, mk, mv, silu_all)


def _merge_kernel(ya_ref, yb_ref, yc_ref, ym_ref, gates_ref, wb_ref, wo_ref, x_ref, g_ref, b_ref,
                  o32_ref, o16_ref, *, alpha):
    d = x_ref.shape[-1]
    mix = None
    for k, y_ref in enumerate((ya_ref, yb_ref, yc_ref, ym_ref)):
        pk = jnp.dot(y_ref[...], wb_ref[k], preferred_element_type=F32)
        pk = pk * gates_ref[:, k * d:(k + 1) * d].astype(F32)
        mix = pk if mix is None else mix + pk
    out = jnp.dot(mix.astype(BF16), wo_ref[...], preferred_element_type=F32)
    z = alpha * x_ref[...] + out
    mu = jnp.mean(z, axis=-1, keepdims=True)
    zc = z - mu
    var = jnp.mean(zc * zc, axis=-1, keepdims=True)
    y = zc * lax.rsqrt(var + EPS) * g_ref[...] + b_ref[...]
    o32_ref[...] = y
    o16_ref[...] = y.astype(BF16)


def _merge(ya, yb, yc, ym, gates, wb, wo, x, ln_g, ln_b, alpha, *, tm_pref=256):
    m, d = x.shape
    tm = _tile(m, tm_pref)
    row = lambda w: pl.BlockSpec((tm, w), lambda i: (i, 0))
    const = lambda shape: pl.BlockSpec(shape, lambda i: (0,) * len(shape))
    return pl.pallas_call(
        functools.partial(_merge_kernel, alpha=alpha),
        grid=(m // tm,),
        in_specs=[row(BR_WIDTH), row(BR_WIDTH), row(BR_WIDTH), row(BR_WIDTH), row(N_BRANCH * d),
                  const((N_BRANCH, BR_WIDTH, d)), const((d, d)), row(d), const((1, d)), const((1, d))],
        out_specs=(row(d), row(d)),
        out_shape=(jax.ShapeDtypeStruct((m, d), F32), jax.ShapeDtypeStruct((m, d), BF16)),
        compiler_params=_params(("parallel",)),
        name="merge",
    )(ya, yb, yc, ym, gates, wb, wo, x, ln_g, ln_b)


def _layer_weights(l, w_in, b_gate, conv_a_w, conv_a_b, ln_a_g, ln_a_b, lam_q1, lam_k1, lam_q2, lam_k2, subln_w,
                   conv_c_w, conv_c_b, dt_bias, a_log, d_skip, norm_c_w, w_mem_kv, w_branch, w_out, ln_g, ln_b):
    w = w_in[l]
    off, cols = 0, {}
    for name, n in (("a_glu", 2 * BR_WIDTH), ("a_gate", BR_WIDTH), ("b_q", BR_WIDTH), ("b_k", KV_HEADS_B * V_DIM_B),
                    ("b_v", KV_HEADS_B * V_DIM_B), ("b_gate", BR_WIDTH), ("c_z", D_INNER_C), ("c_xbc", CONV_C_DIM),
                    ("c_dt", HEADS_C), ("m_q", BR_WIDTH), ("m_gate", BR_WIDTH), ("g", N_BRANCH * w.shape[0])):
        cols[name] = w[:, off:off + n]
        off += n
    d = w.shape[0]
    pad_h = LANES - HEADS_C
    row = lambda v: v.reshape(1, -1).astype(F32)
    return dict(
        w_glu=cols["a_glu"].astype(BF16),
        w_silu=jnp.concatenate([cols["a_gate"], cols["b_gate"], cols["c_z"], cols["m_gate"]], axis=1).astype(BF16),
        w_qm=jnp.concatenate([cols["b_q"] * (HEAD_DIM_B ** -0.5 * LOG2E), cols["m_q"] * HEAD_DIM_M ** -0.5],
                             axis=1).astype(BF16),
        w_kv=jnp.concatenate([cols["b_k"], cols["b_v"]], axis=1).astype(BF16),
        w_xdt=jnp.concatenate([cols["c_xbc"], cols["c_dt"], jnp.zeros((d, pad_h), F32)], axis=1).astype(BF16),
        w_g=cols["g"].astype(BF16),
        b_gate=row(b_gate[l]),
        conv_a_w=conv_a_w[l], conv_a_b=row(conv_a_b[l]), ln_a_g=row(ln_a_g[l]), ln_a_b=row(ln_a_b[l]),
        lamp=jnp.stack([lam_q1[l], lam_k1[l], lam_q2[l], lam_k2[l]]).astype(F32),
        subln_w=row(subln_w[l]),
        conv_c_w=conv_c_w[l], conv_c_b=row(conv_c_b[l]),
        dtb=row(jnp.pad(dt_bias[l], (0, pad_h))), alog=row(jnp.pad(a_log[l], (0, pad_h))),
        dskip_e=row(jnp.repeat(d_skip[l], HEAD_DIM_C)), norm_c_w=row(norm_c_w[l]),
        w_mem_kv=w_mem_kv[l].astype(BF16),
        w_branch=w_branch[l].astype(BF16), w_out=w_out[l].astype(BF16),
        ln_g=row(ln_g[l]), ln_b=row(ln_b[l]),
        lam_init=0.8 - 0.6 * math.exp(-0.3 * l),
    )


def _layer(x32, x16, lw, project_kv, attend, mem, conv_a_state, conv_c_state, ssm_state, expand, alpha):
    bsz, t, d = x32.shape
    m = bsz * t
    xf = x16.reshape(m, d)
    wide = 2 * BR_WIDTH
    u = _proj(xf, lw["w_glu"], epilogue="glu", out_dtype=F32).reshape(bsz, t, BR_WIDTH)
    silu_all = _proj(xf, lw["w_silu"], epilogue="silu", out_dtype=BF16, tn_pref=wide).reshape(bsz, t, 4 * BR_WIDTH)
    qm = _proj(xf, lw["w_qm"], epilogue="none", out_dtype=BF16, tn_pref=wide).reshape(bsz, t, 2 * BR_WIDTH)
    kv32, kv16 = project_kv(xf)
    xdt = _proj(xf, lw["w_xdt"], epilogue="none", out_dtype=F32, tn_pref=CONV_C_DIM + LANES)
    gates = _proj(xf, lw["w_g"], epilogue="sigmoid_bias", out_dtype=BF16, bias=lw["b_gate"], tn_pref=wide)
    kv16 = kv16.reshape(bsz, t, -1)
    xdt = xdt.reshape(bsz, t, CONV_C_DIM + LANES)

    y_a, new_conv_a = _conv(u, conv_a_state, lw["conv_a_w"], lw["conv_a_b"], post="ln_swish_gate", out_dtype=BF16,
                            ln_g=lw["ln_a_g"], ln_b=lw["ln_a_b"], gate=silu_all, gate_col=0)
    y_b = attend(qm, kv32, kv16, silu_all, lw)
    xc, new_conv_c = _conv(xdt, conv_c_state, lw["conv_c_w"], lw["conv_c_b"], post="silu", out_dtype=F32)
    y_c, new_h = _ssd(xc, xdt, CONV_C_DIM // LANES, silu_all, 2, ssm_state, lw["dtb"], lw["alog"], expand,
                      lw["dskip_e"], lw["norm_c_w"])
    y_m = _mem_attn(qm, mem[0], mem[1], mem[2], mem[3], silu_all)
    flat = lambda a: a.reshape(m, -1)
    x32n, x16n = _merge(flat(y_a), flat(y_b), flat(y_c), flat(y_m), gates, lw["w_branch"], lw["w_out"],
                        x32.reshape(m, d), lw["ln_g"], lw["ln_b"], alpha)
    return x32n.reshape(bsz, t, d), x16n.reshape(bsz, t, d), kv32, new_conv_a, new_conv_c, new_h


def kernel(x_prompt, x_sample, mem_prompt, cache_k, cache_v, page_table, cache_mem_k, cache_mem_v, state_conv_a, state_conv_c, state_ssm, w_in, b_gate, conv_a_w, conv_a_b, ln_a_g, ln_a_b, lam_q1, lam_k1, lam_q2, lam_k2, subln_w, conv_c_w, conv_c_b, dt_bias, a_log, d_skip, norm_c_w, w_mem_kv, w_branch, w_out, ln_g, ln_b):
    bsz, seq, d = x_prompt.shape
    db, t_new, _ = x_sample.shape
    depth = w_in.shape[0]
    n_phys, _, page, _, _ = cache_k.shape
    n_mem = mem_prompt.shape[1]
    alpha = (2 * depth) ** 0.25
    kvw = KV_HEADS_B * V_DIM_B

    ck = cache_k.reshape(n_phys, depth, page * KV_HEADS_B, V_DIM_B)
    cv = cache_v.reshape(n_phys, depth, page * KV_HEADS_B, V_DIM_B)
    cmk = cache_mem_k.reshape(db, depth, n_mem, BR_WIDTH)
    cmv = cache_mem_v.reshape(db, depth, n_mem, BR_WIDTH)
    mem16 = mem_prompt.reshape(bsz * n_mem, d).astype(BF16)
    head_of_lane = jnp.arange(D_INNER_C, dtype=jnp.int32) // HEAD_DIM_C
    expand = (jnp.arange(LANES, dtype=jnp.int32)[:, None] == head_of_lane[None, :]).astype(BF16)
    blk_of_row = jnp.arange(2 * KV_HEADS_B * GROUP_B * t_new, dtype=jnp.int32) // (GROUP_B * t_new)
    blk_of_col = jnp.arange(kvw, dtype=jnp.int32) // HEAD_DIM_B
    qmask = (blk_of_row[:, None] == blk_of_col[None, :])

    xp32, xp16 = x_prompt, x_prompt.astype(BF16)
    xs32, xs16 = x_sample, x_sample.astype(BF16)
    outs = {k: [] for k in ("mkp", "mvp", "cap", "ccp", "hp", "ks", "vs", "cas", "ccs", "hs")}
    pages = (jnp.zeros((bsz, seq // page, depth, page * KV_HEADS_B, V_DIM_B), F32),) * 2
    for l in range(depth):
        lw = _layer_weights(l, w_in, b_gate, conv_a_w, conv_a_b, ln_a_g, ln_a_b, lam_q1, lam_k1, lam_q2, lam_k2,
                            subln_w, conv_c_w, conv_c_b, dt_bias, a_log, d_skip, norm_c_w, w_mem_kv, w_branch,
                            w_out, ln_g, ln_b)
        mkv32 = _proj(mem16, lw["w_mem_kv"], epilogue="none", out_dtype=F32)
        mkv_heads = mkv32.reshape(bsz, 1, n_mem, 2 * BR_WIDTH)

        def attend_prompt(qm, kv32, kv16, silu_all, lw):
            return _attn_prompt(qm, kv16, silu_all, lw["lamp"], lw["subln_w"], lw["lam_init"])

        def project_kv_prompt(xf, lw=lw, l=l, pages=pages):
            k_pages, v_pages, kv16 = _proj_kv_pages(xf, lw["w_kv"], pages[0], pages[1], l, seq)
            return (k_pages, v_pages), kv16

        xp32, xp16, pages, ca, cc, hh = _layer(xp32, xp16, lw, project_kv_prompt, attend_prompt,
                                               (mkv_heads, (0, 0), mkv_heads, (0, 1)),
                                               None, None, None, expand, alpha)
        mkv32 = mkv32.reshape(bsz, n_mem, 2 * BR_WIDTH)
        outs["mkp"].append(mkv32[..., :BR_WIDTH].reshape(bsz, n_mem, HEADS_M, HEAD_DIM_M))
        outs["mvp"].append(mkv32[..., BR_WIDTH:].reshape(bsz, n_mem, HEADS_M, HEAD_DIM_M))
        outs["cap"].append(ca); outs["ccp"].append(cc); outs["hp"].append(hh)

        def attend_sample(qm, kv32, kv16, silu_all, lw, l=l):
            q = qm[..., :BR_WIDTH].reshape(db, t_new, KV_HEADS_B, GROUP_B, 2, HEAD_DIM_B)
            q = q.transpose(0, 2, 4, 3, 1, 5).reshape(db, 2 * KV_HEADS_B * GROUP_B * t_new, 1, HEAD_DIM_B)
            qblk = jnp.where(qmask.reshape(1, -1, 2 * KV_HEADS_B, HEAD_DIM_B), q, jnp.zeros((), BF16))
            qblk = qblk.reshape(db, -1, kvw)
            return _attn_sample(qblk, ck, cv, page_table, l, kv32, silu_all, lw["lamp"], lw["subln_w"],
                                lw["lam_init"])

        def project_kv_sample(xf, lw=lw):
            kv32, kv16 = _proj(xf, lw["w_kv"], epilogue="dual")
            return kv32.reshape(db, t_new, 2 * kvw), kv16

        xs32, xs16, kv32, ca, cc, hh = _layer(xs32, xs16, lw, project_kv_sample, attend_sample,
                                               (cmk, (l, 0), cmv, (l, 0)),
                                               state_conv_a[:, l], state_conv_c[:, l], state_ssm[:, l],
                                               expand, alpha)
        outs["ks"].append(kv32[..., :kvw].reshape(db, t_new, KV_HEADS_B, V_DIM_B))
        outs["vs"].append(kv32[..., kvw:].reshape(db, t_new, KV_HEADS_B, V_DIM_B))
        outs["cas"].append(ca); outs["ccs"].append(cc); outs["hs"].append(hh)

    st = lambda k, axis: jnp.stack(outs[k], axis=axis)
    paged = lambda a: a.reshape(bsz, seq // page, depth, page, KV_HEADS_B, V_DIM_B)
    return (xp32, xs32, paged(pages[0]), paged(pages[1]), st("mkp", 1), st("mvp", 1), st("cap", 1), st("ccp", 1),
            st("hp", 1), st("ks", 1), st("vs", 1), st("cas", 1), st("ccs", 1), st("hs", 1))
```

```python
import functools
import math

import jax
import jax.numpy as jnp
from jax import lax
from jax.experimental import pallas as pl
from jax.experimental.pallas import tpu as pltpu

F32 = jnp.float32
BF16 = jnp.bfloat16

LANES = 128
SUBLANES = 8
VMEM_BYTES_V7X = 64 * 1024 * 1024
VMEM_LIMIT = (VMEM_BYTES_V7X * 3) // 4

BR_WIDTH = 1024
N_BRANCH = 4
CONV_A_WIDTH = 31
HEADS_B = 8
KV_HEADS_B = 4
GROUP_B = HEADS_B // KV_HEADS_B
HEAD_DIM_B = 64
V_DIM_B = 2 * HEAD_DIM_B
D_INNER_C = 1024
HEAD_DIM_C = 64
HEADS_C = D_INNER_C // HEAD_DIM_C
GROUPS_C = 4
D_STATE_C = 128
CONV_C_WIDTH = 4
CONV_C_DIM = D_INNER_C + 2 * GROUPS_C * D_STATE_C
SSD_CHUNK = 128
HEADS_M = 4
HEAD_DIM_M = BR_WIDTH // HEADS_M
EPS = 1e-5
NEG_INF = float("-inf")
LOG2E = math.log2(math.e)


def _tile(n, pref):
    if n <= pref:
        return n
    t = pref
    while n % t:
        t //= 2
    return t


def _sigmoid(x):
    return 1.0 / (1.0 + jnp.exp(-x))


def _silu(x):
    return x * _sigmoid(x)


def _params(sem):
    return pltpu.CompilerParams(dimension_semantics=sem, vmem_limit_bytes=VMEM_LIMIT)


def _proj_kernel(*refs, epilogue):
    x = refs[0][...]
    if epilogue == "glu":
        _, w1_ref, w2_ref, o_ref = refs
        a1 = jnp.dot(x, w1_ref[...], preferred_element_type=F32)
        a2 = jnp.dot(x, w2_ref[...], preferred_element_type=F32)
        o_ref[...] = (a1 * _sigmoid(a2)).astype(o_ref.dtype)
    elif epilogue == "sigmoid_bias":
        _, w_ref, b_ref, o_ref = refs
        y = jnp.dot(x, w_ref[...], preferred_element_type=F32)
        o_ref[...] = _sigmoid(y + b_ref[...]).astype(o_ref.dtype)
    elif epilogue == "dual":
        _, w_ref, o32_ref, o16_ref = refs
        y = jnp.dot(x, w_ref[...], preferred_element_type=F32)
        o32_ref[...] = y
        o16_ref[...] = y.astype(BF16)
    elif epilogue == "kv_pages":
        _, w_ref, _, _, ko_ref, vo_ref, o16_ref = refs
        y = jnp.dot(x, w_ref[...], preferred_element_type=F32)
        o16_ref[...] = y.astype(BF16)
        page = ko_ref.shape[3] // KV_HEADS_B
        kvw = KV_HEADS_B * V_DIM_B
        for j in range(ko_ref.shape[1]):
            for h in range(KV_HEADS_B):
                rows = pl.ds(h, page, stride=KV_HEADS_B)
                ko_ref[0, j, 0, rows, :] = y[j * page:(j + 1) * page, h * V_DIM_B:(h + 1) * V_DIM_B]
                vo_ref[0, j, 0, rows, :] = y[j * page:(j + 1) * page, kvw + h * V_DIM_B:kvw + (h + 1) * V_DIM_B]
    else:
        _, w_ref, o_ref = refs
        y = jnp.dot(x, w_ref[...], preferred_element_type=F32)
        if epilogue == "silu":
            y = _silu(y)
        o_ref[...] = y.astype(o_ref.dtype)


def _proj(x, w, *, epilogue, out_dtype=F32, bias=None, tn_pref=1024, tm_pref=1024):
    m, k = x.shape
    n = w.shape[1] // 2 if epilogue == "glu" else w.shape[1]
    tm, tn = _tile(m, tm_pref), _tile(n, tn_pref)
    nj = n // tn
    x_spec = pl.BlockSpec((tm, k), lambda j, i: (i, 0))
    w_spec = pl.BlockSpec((k, tn), lambda j, i: (0, j))
    o_spec = pl.BlockSpec((tm, tn), lambda j, i: (i, j))
    in_specs, args = [x_spec, w_spec], [x, w]
    out_shape, out_specs = jax.ShapeDtypeStruct((m, n), out_dtype), o_spec
    if epilogue == "glu":
        in_specs.append(pl.BlockSpec((k, tn), lambda j, i: (0, j + nj)))
        args.append(w)
    elif epilogue == "sigmoid_bias":
        in_specs.append(pl.BlockSpec((1, tn), lambda j, i: (0, j)))
        args.append(bias)
    elif epilogue == "dual":
        out_shape = (jax.ShapeDtypeStruct((m, n), F32), jax.ShapeDtypeStruct((m, n), BF16))
        out_specs = (o_spec, o_spec)
    return pl.pallas_call(
        functools.partial(_proj_kernel, epilogue=epilogue),
        grid=(nj, m // tm),
        in_specs=in_specs,
        out_specs=out_specs,
        out_shape=out_shape,
        compiler_params=_params(("parallel", "parallel")),
        name="proj_" + epilogue,
    )(*args)


def _proj_kv_pages(x, w, k_pages, v_pages, layer, seq, *, tm_pref=1024):
    m, k = x.shape
    n = w.shape[1]
    page = k_pages.shape[3] // KV_HEADS_B
    tm = _tile(seq, tm_pref)
    per_seq = seq // tm
    pg_spec = pl.BlockSpec((1, tm // page, 1, page * KV_HEADS_B, V_DIM_B),
                           lambda j, i: (i // per_seq, i % per_seq, layer, 0, 0))
    return pl.pallas_call(
        functools.partial(_proj_kernel, epilogue="kv_pages"),
        grid=(1, m // tm),
        in_specs=[pl.BlockSpec((tm, k), lambda j, i: (i, 0)), pl.BlockSpec((k, n), lambda j, i: (0, 0)),
                  pl.BlockSpec(memory_space=pl.ANY), pl.BlockSpec(memory_space=pl.ANY)],
        out_specs=(pg_spec, pg_spec, pl.BlockSpec((tm, n), lambda j, i: (i, 0))),
        out_shape=(jax.ShapeDtypeStruct(k_pages.shape, F32), jax.ShapeDtypeStruct(v_pages.shape, F32),
                   jax.ShapeDtypeStruct((m, n), BF16)),
        input_output_aliases={2: 0, 3: 1},
        compiler_params=_params(("parallel", "parallel")),
        name="proj_kv_pages",
    )(x, w, k_pages, v_pages)


def _conv_kernel(*refs, taps, hist, tt, rb, post, has_state):
    refs = list(refs)
    x_ref = refs.pop(0)
    state_ref = refs.pop(0) if has_state else None
    w_ref, b_ref = refs.pop(0), refs.pop(0)
    if post == "ln_swish_gate":
        g_ref, beta_ref, gate_ref = refs.pop(0), refs.pop(0), refs.pop(0)
    y_ref, ns_ref, win_ref, sh_ref = refs
    t = pl.program_id(1)
    back = taps - 1
    c = x_ref.shape[-1]
    n_win = hist + tt

    @pl.when(t == 0)
    def _():
        win_ref[0:hist, :] = jnp.zeros((hist, c), F32)
        win_ref[n_win:n_win + SUBLANES, :] = jnp.zeros((SUBLANES, c), F32)
        if has_state:
            win_ref[hist - back:hist, :] = state_ref[0].astype(F32)

    win_ref[hist:n_win, :] = x_ref[0].astype(F32)

    offsets = [j + hist - back for j in range(taps)]
    for p in sorted({o % SUBLANES for o in offsets} - {0}):
        sh_ref[p - 1, :, :] = win_ref[p:p + n_win, :]

    for r0 in range(0, tt, rb):
        acc = jnp.broadcast_to(b_ref[...], (rb, c))
        for j, o in enumerate(offsets):
            a, p = divmod(o, SUBLANES)
            lo = r0 + a * SUBLANES
            xs = win_ref[lo:lo + rb, :] if p == 0 else sh_ref[p - 1, lo:lo + rb, :]
            wj = w_ref[j * SUBLANES:(j + 1) * SUBLANES, :]
            acc = acc + (wj if rb == SUBLANES else jnp.concatenate([wj] * (rb // SUBLANES), axis=0)) * xs
        if post == "ln_swish_gate":
            mu = jnp.mean(acc, axis=-1, keepdims=True)
            d = acc - mu
            var = jnp.mean(d * d, axis=-1, keepdims=True)
            y = d * lax.rsqrt(var + EPS) * g_ref[...] + beta_ref[...]
            y = _silu(y) * gate_ref[0, r0:r0 + rb, :].astype(F32)
        else:
            y = _silu(acc)
        y_ref[0, r0:r0 + rb, :] = y.astype(y_ref.dtype)

    @pl.when(t == pl.num_programs(1) - 1)
    def _():
        ns_ref[0] = win_ref[n_win - back:n_win, :]

    win_ref[0:hist, :] = win_ref[tt:tt + hist, :]


def _conv(x, state, w, b, *, post, out_dtype, ln_g=None, ln_b=None, gate=None, gate_col=0, tt_pref=256):
    bsz, t, _ = x.shape
    taps, c = w.shape
    back = taps - 1
    hist = -(-back // SUBLANES) * SUBLANES
    tt = _tile(t, tt_pref)
    rb = min(tt, max(SUBLANES, (32 * 1024) // c))
    has_state = state is not None
    in_specs = [pl.BlockSpec((1, tt, c), lambda i, j: (i, j, 0))]
    args = [x]
    if has_state:
        in_specs.append(pl.BlockSpec((1, back, c), lambda i, j: (i, 0, 0)))
        args.append(state)
    in_specs += [pl.BlockSpec((taps * SUBLANES, c), lambda i, j: (0, 0)), pl.BlockSpec((1, c), lambda i, j: (0, 0))]
    args += [jnp.repeat(w, SUBLANES, axis=0), b]
    if post == "ln_swish_gate":
        in_specs += [pl.BlockSpec((1, c), lambda i, j: (0, 0)), pl.BlockSpec((1, c), lambda i, j: (0, 0)),
                     pl.BlockSpec((1, tt, c), lambda i, j: (i, j, gate_col))]
        args += [ln_g, ln_b, gate]
    return pl.pallas_call(
        functools.partial(_conv_kernel, taps=taps, hist=hist, tt=tt, rb=rb, post=post, has_state=has_state),
        grid=(bsz, t // tt),
        in_specs=in_specs,
        out_specs=(pl.BlockSpec((1, tt, c), lambda i, j: (i, j, 0)),
                   pl.BlockSpec((1, back, c), lambda i, j: (i, 0, 0))),
        out_shape=(jax.ShapeDtypeStruct((bsz, t, c), out_dtype), jax.ShapeDtypeStruct((bsz, back, c), F32)),
        scratch_shapes=[pltpu.VMEM((hist + tt + SUBLANES, c), F32),
                        pltpu.VMEM((SUBLANES - 1, hist + tt, c), F32)],
        compiler_params=_params(("parallel", "arbitrary")),
        name="conv_" + post,
    )(*args)


def _split3(x):
    hi = x.astype(BF16)
    r1 = x - hi.astype(F32)
    mid = r1.astype(BF16)
    lo = (r1 - mid.astype(F32)).astype(BF16)
    return hi, mid, lo


def _dot01(a01, x, *, left):
    out = None
    for piece in _split3(x):
        y = (jnp.dot(a01, piece, preferred_element_type=F32) if left
             else jnp.dot(piece, a01, preferred_element_type=F32))
        out = y if out is None else out + y
    return out


def _ssd_kernel(*refs, q, t_valid, has_h0):
    refs = list(refs)
    xc_ref, dt_ref, sz_ref = refs.pop(0), refs.pop(0), refs.pop(0)
    h0_ref = refs.pop(0) if has_h0 else None
    dtb_ref, alog_ref, e_ref, dskip_ref, nw_ref, y_ref, hout_ref, ht_ref = refs
    ci = pl.program_id(1)
    rows = xc_ref.shape[1]
    nstate = GROUPS_C * D_STATE_C

    @pl.when(ci == 0)
    def _():
        if has_h0:
            ht_ref[...] = h0_ref[0].reshape(D_INNER_C, D_STATE_C).T
        else:
            ht_ref[...] = jnp.zeros(ht_ref.shape, F32)

    xc = xc_ref[0]
    dtr = dt_ref[0]
    if rows < q:
        xc = jnp.concatenate([xc, jnp.zeros((q - rows, xc.shape[1]), F32)], axis=0)
        dtr = jnp.concatenate([dtr, jnp.zeros((q - rows, dtr.shape[1]), F32)], axis=0)
    x = xc[:, :D_INNER_C]
    bm = xc[:, D_INNER_C:D_INNER_C + nstate]
    cm = xc[:, D_INNER_C + nstate:]

    z = dtr + dtb_ref[...]
    dt = jnp.maximum(z, 0.0) + jnp.log1p(jnp.exp(-jnp.abs(z)))
    row_id = lax.broadcasted_iota(jnp.int32, (q, LANES), 0)
    if t_valid < q:
        dt = jnp.where(row_id < t_valid, dt, 0.0)
    a_neg = -jnp.exp(alog_ref[...])
    r_i = lax.broadcasted_iota(jnp.int32, (q, q), 0)
    c_i = lax.broadcasted_iota(jnp.int32, (q, q), 1)
    causal = r_i >= c_i
    tril = jnp.where(causal, 1.0, 0.0).astype(BF16)
    cum = _dot01(tril, dt * a_neg, left=True)
    cum_last = cum[q - 1:q, :]
    expand = e_ref[...]
    dt_e = _dot01(expand, dt, left=False)
    ecum_e = _dot01(expand, jnp.exp(cum), left=False)
    toend_e = _dot01(expand, jnp.exp(cum_last - cum), left=False)
    cum_t = cum.T
    xdt = x * dt_e
    xdt16 = xdt.astype(BF16)
    xdte16 = (xdt * toend_e).astype(BF16)
    lane = lax.broadcasted_iota(jnp.int32, (q, LANES), 1)

    hg = HEADS_C // GROUPS_C
    gw = hg * HEAD_DIM_C
    ys = []
    for g in range(GROUPS_C):
        bg = bm[:, g * D_STATE_C:(g + 1) * D_STATE_C]
        cg16 = cm[:, g * D_STATE_C:(g + 1) * D_STATE_C].astype(BF16)
        bgt16 = bg.T.astype(BF16)
        cb = jnp.dot(cg16, bgt16, preferred_element_type=F32)
        h_prev = ht_ref[:, g * gw:(g + 1) * gw]
        y_inter = jnp.dot(cg16, h_prev.astype(BF16), preferred_element_type=F32) * ecum_e[:, g * gw:(g + 1) * gw]
        pairs = []
        for k in range(hg // 2):
            xp = xdt16[:, g * gw + k * LANES:g * gw + (k + 1) * LANES]
            halves = []
            for hh in range(2):
                h = g * hg + 2 * k + hh
                seg = cum[:, h:h + 1] - cum_t[h:h + 1, :]
                dec = jnp.exp(jnp.where(causal, seg, NEG_INF))
                halves.append(jnp.dot((cb * dec).astype(BF16), xp, preferred_element_type=F32))
            pairs.append(jnp.where(lane < HEAD_DIM_C, halves[0], halves[1]))
        ys.append(jnp.concatenate(pairs, axis=1) + y_inter)
        ht_ref[:, g * gw:(g + 1) * gw] = (
            h_prev * ecum_e[q - 1:q, g * gw:(g + 1) * gw]
            + jnp.dot(bgt16, xdte16[:, g * gw:(g + 1) * gw], preferred_element_type=F32))
    y = jnp.concatenate(ys, axis=1) + dskip_ref[...] * x
    y = y * sz_ref[0].astype(F32) if rows == q else y[:rows] * sz_ref[0].astype(F32)
    y = y * lax.rsqrt(jnp.mean(y * y, axis=-1, keepdims=True) + EPS) * nw_ref[...]
    y_ref[0] = y.astype(y_ref.dtype)

    @pl.when(ci == pl.num_programs(1) - 1)
    def _():
        hout_ref[0] = ht_ref[...].T.reshape(HEADS_C, HEAD_DIM_C, D_STATE_C)


def _ssd(xc, dt_raw, dt_col, sz, sz_col, h0, dtb, alog, expand, dskip_e, normw):
    bsz, t, _ = xc.shape
    q = SSD_CHUNK
    rows = q if t % q == 0 else t
    assert rows <= q and rows % SUBLANES == 0
    nc = t // rows
    has_h0 = h0 is not None
    in_specs = [pl.BlockSpec((1, rows, CONV_C_DIM), lambda b, c: (b, c, 0)),
                pl.BlockSpec((1, rows, LANES), lambda b, c: (b, c, dt_col)),
                pl.BlockSpec((1, rows, D_INNER_C), lambda b, c: (b, c, sz_col))]
    args = [xc, dt_raw, sz]
    if has_h0:
        in_specs.append(pl.BlockSpec((1, HEADS_C, HEAD_DIM_C, D_STATE_C), lambda b, c: (b, 0, 0, 0)))
        args.append(h0)
    in_specs += [pl.BlockSpec((1, LANES), lambda b, c: (0, 0)), pl.BlockSpec((1, LANES), lambda b, c: (0, 0)),
                 pl.BlockSpec((LANES, D_INNER_C), lambda b, c: (0, 0)),
                 pl.BlockSpec((1, D_INNER_C), lambda b, c: (0, 0)), pl.BlockSpec((1, D_INNER_C), lambda b, c: (0, 0))]
    args += [dtb, alog, expand, dskip_e, normw]
    return pl.pallas_call(
        functools.partial(_ssd_kernel, q=q, t_valid=rows, has_h0=has_h0),
        grid=(bsz, nc),
        in_specs=in_specs,
        out_specs=(pl.BlockSpec((1, rows, D_INNER_C), lambda b, c: (b, c, 0)),
                   pl.BlockSpec((1, HEADS_C, HEAD_DIM_C, D_STATE_C), lambda b, c: (b, 0, 0, 0))),
        out_shape=(jax.ShapeDtypeStruct((bsz, t, D_INNER_C), BF16),
                   jax.ShapeDtypeStruct((bsz, HEADS_C, HEAD_DIM_C, D_STATE_C), F32)),
        scratch_shapes=[pltpu.VMEM((D_STATE_C, D_INNER_C), F32)],
        compiler_params=_params(("parallel", "arbitrary")),
        name="ssd",
    )(*args)


def _lam(lamp_ref, lam_init):
    p = lamp_ref[...]
    s1 = jnp.sum(p[0:1] * p[1:2], axis=-1, keepdims=True)
    s2 = jnp.sum(p[2:3] * p[3:4], axis=-1, keepdims=True)
    return jnp.exp(s1) - jnp.exp(s2) + lam_init


def _diff_finish(o0, o1, lam, subw, lam_init, gate):
    o = o0 - lam * o1
    o = o * lax.rsqrt(jnp.mean(o * o, axis=-1, keepdims=True) + EPS) * subw
    return o * (1.0 - lam_init) * gate


def _online_softmax_step(s, v16, m_ref, l_ref, acc_ref):
    m_prev = m_ref[...]
    m_new = jnp.maximum(m_prev, jnp.max(s, axis=-1, keepdims=True))
    p = jnp.exp2(s - m_new)
    alpha = jnp.exp2(m_prev - m_new)
    l_ref[...] = alpha * l_ref[...] + jnp.sum(p, axis=-1, keepdims=True)
    acc_ref[...] = alpha * acc_ref[...] + jnp.dot(p.astype(BF16), v16, preferred_element_type=F32)
    m_ref[...] = m_new


def _lane_tile(x, width):
    n = width // LANES
    return x if n == 1 else jnp.concatenate([x] * n, axis=1)


def _attn_prompt_kernel(q_ref, k_ref, v_ref, gate_ref, lamp_ref, subw_ref, o_ref,
                        qs_ref, v1_ref, m_ref, acc_ref, p_ref, al_ref, *, tq, tk, lam_init):
    qi = pl.program_id(2)
    nblk = 2 * GROUP_B
    nt = (((1,), (1,)), ((), ()))

    @pl.when(qi == 0)
    def _():
        v1_ref[:, :V_DIM_B] = v_ref[0]
        v1_ref[:, V_DIM_B:] = jnp.ones((v1_ref.shape[0], V_DIM_B), BF16)

    lane = lax.broadcasted_iota(jnp.int32, (tq, LANES), 1)
    for g in range(GROUP_B):
        qg = q_ref[0, :, g * LANES:(g + 1) * LANES]
        for c in range(2):
            keep = (lane >= HEAD_DIM_B) if c else (lane < HEAD_DIM_B)
            qs_ref[(2 * g + c) * tq:(2 * g + c + 1) * tq, :] = jnp.where(keep, qg, jnp.zeros_like(qg))
    m_ref[...] = jnp.full(m_ref.shape, NEG_INF, F32)
    acc_ref[...] = jnp.zeros(acc_ref.shape, F32)

    def step(start, width, diagonal):
        kt = k_ref[0, pl.ds(start, width), :]
        vt = v1_ref[pl.ds(start, width), :]
        if diagonal:
            keep = (lax.broadcasted_iota(jnp.int32, (tq, width), 0)
                    >= lax.broadcasted_iota(jnp.int32, (tq, width), 1))
        for i in range(nblk):
            rows = slice(i * tq, (i + 1) * tq)
            s = lax.dot_general(qs_ref[rows, :], kt, nt, preferred_element_type=F32)
            if diagonal:
                s = jnp.where(keep, s, NEG_INF)
            m_prev = m_ref[rows, :]
            m_new = jnp.maximum(m_prev, jnp.max(s, axis=-1, keepdims=True))
            p = jnp.exp2(s - _lane_tile(m_new, width))
            alpha = jnp.exp2(m_prev - m_new)
            pv = jnp.dot(p.astype(BF16), vt, preferred_element_type=F32)
            acc_ref[rows, :] = _lane_tile(alpha, 2 * V_DIM_B) * acc_ref[rows, :] + pv
            m_ref[rows, :] = m_new

    q0 = qi * tq
    n_wide = q0 // tk

    def wide_scores(rows, kt):
        s = lax.dot_general(qs_ref[rows, :], kt, nt, preferred_element_type=F32)
        m_prev = m_ref[rows, :]
        m_new = jnp.maximum(m_prev, jnp.max(s, axis=-1, keepdims=True))
        m_ref[rows, :] = m_new
        p_ref[rows, :] = jnp.exp2(s - _lane_tile(m_new, tk)).astype(BF16)
        al_ref[rows, :] = jnp.exp2(m_prev - m_new)

    def wide_values(rows, vt):
        pv = jnp.dot(p_ref[rows, :], vt, preferred_element_type=F32)
        acc_ref[rows, :] = _lane_tile(al_ref[rows, :], 2 * V_DIM_B) * acc_ref[rows, :] + pv

    @pl.when(n_wide > 0)
    def _():
        kt = k_ref[0, pl.ds(0, tk), :]
        for i in range(nblk):
            wide_scores(slice(i * tq, (i + 1) * tq), kt)

    def wide_body(ki, carry):
        kt = k_ref[0, pl.ds(pl.multiple_of(ki * tk, tk), tk), :]
        vt_prev = v1_ref[pl.ds(pl.multiple_of((ki - 1) * tk, tk), tk), :]
        for i in range(nblk):
            rows = slice(i * tq, (i + 1) * tq)
            wide_values(rows, vt_prev)
            wide_scores(rows, kt)
        return carry

    lax.fori_loop(1, n_wide, wide_body, 0)

    @pl.when(n_wide > 0)
    def _():
        vt_last = v1_ref[pl.ds(pl.multiple_of((n_wide - 1) * tk, tk), tk), :]
        for i in range(nblk):
            wide_values(slice(i * tq, (i + 1) * tq), vt_last)

    def narrow_body(j, carry):
        step(pl.multiple_of(n_wide * tk + j * tq, tq), tq, False)
        return carry

    if tk > tq:
        lax.fori_loop(0, (q0 - n_wide * tk) // tq, narrow_body, 0)
    step(pl.multiple_of(q0, tq), tq, True)

    lam = _lam(lamp_ref, lam_init)
    for g in range(GROUP_B):
        r0, r1 = (2 * g) * tq, (2 * g + 1) * tq
        o0 = acc_ref[r0:r0 + tq, :V_DIM_B] / acc_ref[r0:r0 + tq, V_DIM_B:]
        o1 = acc_ref[r1:r1 + tq, :V_DIM_B] / acc_ref[r1:r1 + tq, V_DIM_B:]
        gate = gate_ref[0, :, g * LANES:(g + 1) * LANES].astype(F32)
        o_ref[0, :, g * LANES:(g + 1) * LANES] = _diff_finish(
            o0, o1, lam, subw_ref[...], lam_init, gate).astype(o_ref.dtype)


def _attn_prompt(qm, kv16, silu_all, lamp, subw, lam_init, *, tq_pref=256, tk_pref=512):
    bsz, s, _ = kv16.shape
    tq = _tile(s, tq_pref)
    tk = _tile(s, tk_pref)
    assert tk % tq == 0
    hw = GROUP_B * V_DIM_B
    gate_blk0 = BR_WIDTH // hw
    nrow = 2 * GROUP_B * tq
    return pl.pallas_call(
        functools.partial(_attn_prompt_kernel, tq=tq, tk=tk, lam_init=lam_init),
        grid=(bsz, KV_HEADS_B, s // tq),
        in_specs=[pl.BlockSpec((1, tq, hw), lambda b, h, i: (b, i, h)),
                  pl.BlockSpec((1, s, V_DIM_B), lambda b, h, i: (b, 0, h)),
                  pl.BlockSpec((1, s, V_DIM_B), lambda b, h, i: (b, 0, KV_HEADS_B + h)),
                  pl.BlockSpec((1, tq, hw), lambda b, h, i: (b, i, gate_blk0 + h)),
                  pl.BlockSpec((4, HEAD_DIM_B), lambda b, h, i: (0, 0)),
                  pl.BlockSpec((1, V_DIM_B), lambda b, h, i: (0, 0))],
        out_specs=pl.BlockSpec((1, tq, hw), lambda b, h, i: (b, i, h)),
        out_shape=jax.ShapeDtypeStruct((bsz, s, BR_WIDTH), BF16),
        scratch_shapes=[pltpu.VMEM((nrow, LANES), BF16),
                        pltpu.VMEM((s, 2 * V_DIM_B), BF16),
                        pltpu.VMEM((nrow, LANES), F32),
                        pltpu.VMEM((nrow, 2 * V_DIM_B), F32),
                        pltpu.VMEM((nrow, tk), BF16),
                        pltpu.VMEM((nrow, LANES), F32)],
        compiler_params=_params(("parallel", "parallel", "arbitrary")),
        name="attn_prompt",
    )(qm, kv16, kv16, silu_all, lamp, subw)


def _attn_sample_kernel(*refs, pps, t_new, lam_init):
    pt_ref = refs[0]
    del pt_ref
    qb_ref = refs[1]
    k_refs = refs[2:2 + pps]
    v_refs = refs[2 + pps:2 + 2 * pps]
    kn_ref, vn_ref, gate_ref, lamp_ref, subw_ref, o_ref, m_ref, l_ref, acc_ref = refs[2 + 2 * pps:]
    si = pl.program_id(1)
    nrow = qb_ref.shape[1]
    kvw = KV_HEADS_B * V_DIM_B

    @pl.when(si == 0)
    def _():
        m_ref[...] = jnp.full(m_ref.shape, NEG_INF, F32)
        l_ref[...] = jnp.zeros(l_ref.shape, F32)
        acc_ref[...] = jnp.zeros(acc_ref.shape, F32)

    qb = qb_ref[0]
    nt = (((1,), (1,)), ((), ()))
    page = k_refs[0].shape[2] // KV_HEADS_B

    def pages16(page_refs):
        return jnp.concatenate(
            [jnp.concatenate([r[0, 0, pl.ds(h, page, stride=KV_HEADS_B), :] for h in range(KV_HEADS_B)],
                             axis=1).astype(BF16) for r in page_refs],
            axis=0)

    k16 = pages16(k_refs)
    v16 = pages16(v_refs)
    s = lax.dot_general(qb, k16, nt, preferred_element_type=F32)
    _online_softmax_step(s, v16, m_ref, l_ref, acc_ref)

    @pl.when(si == pl.num_programs(1) - 1)
    def _():
        pad = jnp.zeros((LANES - t_new, kvw), BF16)
        kn = jnp.concatenate([kn_ref[0].astype(BF16), pad], axis=0)
        vn = jnp.concatenate([vn_ref[0].astype(BF16), pad], axis=0)
        sn = lax.dot_general(qb, kn, nt, preferred_element_type=F32)
        tok = lax.broadcasted_iota(jnp.int32, (nrow, LANES), 0) % t_new
        col = lax.broadcasted_iota(jnp.int32, (nrow, LANES), 1)
        sn = jnp.where(col <= tok, sn, NEG_INF)
        _online_softmax_step(sn, vn, m_ref, l_ref, acc_ref)
        lam = _lam(lamp_ref, lam_init)
        for h in range(KV_HEADS_B):
            for g in range(GROUP_B):
                r0 = ((h * 2 + 0) * GROUP_B + g) * t_new
                r1 = ((h * 2 + 1) * GROUP_B + g) * t_new
                cs = slice(h * V_DIM_B, (h + 1) * V_DIM_B)
                o0 = acc_ref[r0:r0 + t_new, cs] / l_ref[r0:r0 + t_new, :]
                o1 = acc_ref[r1:r1 + t_new, cs] / l_ref[r1:r1 + t_new, :]
                oc = slice((h * GROUP_B + g) * V_DIM_B, (h * GROUP_B + g + 1) * V_DIM_B)
                gate = gate_ref[0, :, oc].astype(F32)
                o_ref[0, :, oc] = _diff_finish(o0, o1, lam, subw_ref[...], lam_init, gate).astype(o_ref.dtype)


def _attn_sample(qblk, cache_k, cache_v, page_table, layer, kv32, silu_all, lamp, subw, lam_init):
    db, nrow, kvw = qblk.shape
    t_new = kv32.shape[1]
    page_rows = cache_k.shape[2]
    n_pages = page_table.shape[1]
    pps = _tile(n_pages, 16)
    steps = n_pages // pps

    def page_spec(i):
        return pl.BlockSpec((1, 1, page_rows, V_DIM_B), lambda b, s, pt: (pt[b, s * pps + i], layer, 0, 0))

    in_specs = [pl.BlockSpec((1, nrow, kvw), lambda b, s, pt: (b, 0, 0))]
    in_specs += [page_spec(i) for i in range(pps)] + [page_spec(i) for i in range(pps)]
    in_specs += [pl.BlockSpec((1, t_new, kvw), lambda b, s, pt: (b, 0, 0)),
                 pl.BlockSpec((1, t_new, kvw), lambda b, s, pt: (b, 0, 1)),
                 pl.BlockSpec((1, t_new, BR_WIDTH), lambda b, s, pt: (b, 0, 1)),
                 pl.BlockSpec((4, HEAD_DIM_B), lambda b, s, pt: (0, 0)),
                 pl.BlockSpec((1, V_DIM_B), lambda b, s, pt: (0, 0))]
    grid_spec = pltpu.PrefetchScalarGridSpec(
        num_scalar_prefetch=1,
        grid=(db, steps),
        in_specs=in_specs,
        out_specs=pl.BlockSpec((1, t_new, BR_WIDTH), lambda b, s, pt: (b, 0, 0)),
        scratch_shapes=[pltpu.VMEM((nrow, 1), F32), pltpu.VMEM((nrow, 1), F32), pltpu.VMEM((nrow, kvw), F32)])
    return pl.pallas_call(
        functools.partial(_attn_sample_kernel, pps=pps, t_new=t_new, lam_init=lam_init),
        grid_spec=grid_spec,
        out_shape=jax.ShapeDtypeStruct((db, t_new, BR_WIDTH), BF16),
        compiler_params=_params(("parallel", "arbitrary")),
        name="attn_sample",
    )(page_table, qblk, *([cache_k] * pps), *([cache_v] * pps), kv32, kv32, silu_all, lamp, subw)


def _mem_attn_kernel(q_ref, mk_ref, mv_ref, gate_ref, o_ref):
    nt = (((1,), (1,)), ((), ()))
    for h in range(HEADS_M):
        cs = slice(h * HEAD_DIM_M, (h + 1) * HEAD_DIM_M)
        s = lax.dot_general(q_ref[0, :, cs], mk_ref[0, 0, :, cs].astype(BF16), nt, preferred_element_type=F32)
        p = jnp.exp(s - jnp.max(s, axis=-1, keepdims=True))
        p = p / jnp.sum(p, axis=-1, keepdims=True)
        o = jnp.dot(p.astype(BF16), mv_ref[0, 0, :, cs].astype(BF16), preferred_element_type=F32)
        o_ref[0, :, cs] = (o * gate_ref[0, :, cs].astype(F32)).astype(o_ref.dtype)


def _mem_attn(qm, mk, mk_idx, mv, mv_idx, silu_all, *, tq_pref=512):
    bsz, t, _ = qm.shape
    n_mem = mk.shape[2]
    tq = _tile(t, tq_pref)
    return pl.pallas_call(
        _mem_attn_kernel,
        grid=(bsz, t // tq),
        in_specs=[pl.BlockSpec((1, tq, BR_WIDTH), lambda b, i: (b, i, 1)),
                  pl.BlockSpec((1, 1, n_mem, BR_WIDTH), lambda b, i: (b, mk_idx[0], 0, mk_idx[1])),
                  pl.BlockSpec((1, 1, n_mem, BR_WIDTH), lambda b, i: (b, mv_idx[0], 0, mv_idx[1])),
                  pl.BlockSpec((1, tq, BR_WIDTH), lambda b, i: (b, i, 3))],
        out_specs=pl.BlockSpec((1, tq, BR_WIDTH), lambda b, i: (b, i, 0)),
        out_shape=jax.ShapeDtypeStruct((bsz, t, BR_WIDTH), BF16),
        compiler_params=_params(("parallel", "parallel")),
        name="mem_attn",
    )(qm, mk, mv, silu_all)


def _merge_kernel(ya_ref, yb_ref, yc_ref, ym_ref, gates_ref, wb_ref, wo_ref, x_ref, g_ref, b_ref,
                  o32_ref, o16_ref, *, alpha):
    d = x_ref.shape[-1]
    mix = None
    for k, y_ref in enumerate((ya_ref, yb_ref, yc_ref, ym_ref)):
        pk = jnp.dot(y_ref[...], wb_ref[k], preferred_element_type=F32)
        pk = pk * gates_ref[:, k * d:(k + 1) * d].astype(F32)
        mix = pk if mix is None else mix + pk
    out = jnp.dot(mix.astype(BF16), wo_ref[...], preferred_element_type=F32)
    z = alpha * x_ref[...] + out
    mu = jnp.mean(z, axis=-1, keepdims=True)
    zc = z - mu
    var = jnp.mean(zc * zc, axis=-1, keepdims=True)
    y = zc * lax.rsqrt(var + EPS) * g_ref[...] + b_ref[...]
    o32_ref[...] = y
    o16_ref[...] = y.astype(BF16)


def _merge(ya, yb, yc, ym, gates, wb, wo, x, ln_g, ln_b, alpha, *, tm_pref=256):
    m, d = x.shape
    tm = _tile(m, tm_pref)
    row = lambda w: pl.BlockSpec((tm, w), lambda i: (i, 0))
    const = lambda shape: pl.BlockSpec(shape, lambda i: (0,) * len(shape))
    return pl.pallas_call(
        functools.partial(_merge_kernel, alpha=alpha),
        grid=(m // tm,),
        in_specs=[row(BR_WIDTH), row(BR_WIDTH), row(BR_WIDTH), row(BR_WIDTH), row(N_BRANCH * d),
                  const((N_BRANCH, BR_WIDTH, d)), const((d, d)), row(d), const((1, d)), const((1, d))],
        out_specs=(row(d), row(d)),
        out_shape=(jax.ShapeDtypeStruct((m, d), F32), jax.ShapeDtypeStruct((m, d), BF16)),
        compiler_params=_params(("parallel",)),
        name="merge",
    )(ya, yb, yc, ym, gates, wb, wo, x, ln_g, ln_b)


def _layer_weights(l, w_in, b_gate, conv_a_w, conv_a_b, ln_a_g, ln_a_b, lam_q1, lam_k1, lam_q2, lam_k2, subln_w,
                   conv_c_w, conv_c_b, dt_bias, a_log, d_skip, norm_c_w, w_mem_kv, w_branch, w_out, ln_g, ln_b):
    w = w_in[l]
    off, cols = 0, {}
    for name, n in (("a_glu", 2 * BR_WIDTH), ("a_gate", BR_WIDTH), ("b_q", BR_WIDTH), ("b_k", KV_HEADS_B * V_DIM_B),
                    ("b_v", KV_HEADS_B * V_DIM_B), ("b_gate", BR_WIDTH), ("c_z", D_INNER_C), ("c_xbc", CONV_C_DIM),
                    ("c_dt", HEADS_C), ("m_q", BR_WIDTH), ("m_gate", BR_WIDTH), ("g", N_BRANCH * w.shape[0])):
        cols[name] = w[:, off:off + n]
        off += n
    d = w.shape[0]
    pad_h = LANES - HEADS_C
    row = lambda v: v.reshape(1, -1).astype(F32)
    return dict(
        w_glu=cols["a_glu"].astype(BF16),
        w_silu=jnp.concatenate([cols["a_gate"], cols["b_gate"], cols["c_z"], cols["m_gate"]], axis=1).astype(BF16),
        w_qm=jnp.concatenate([cols["b_q"] * (HEAD_DIM_B ** -0.5 * LOG2E), cols["m_q"] * HEAD_DIM_M ** -0.5],
                             axis=1).astype(BF16),
        w_kv=jnp.concatenate([cols["b_k"], cols["b_v"]], axis=1).astype(BF16),
        w_xdt=jnp.concatenate([cols["c_xbc"], cols["c_dt"], jnp.zeros((d, pad_h), F32)], axis=1).astype(BF16),
        w_g=cols["g"].astype(BF16),
        b_gate=row(b_gate[l]),
        conv_a_w=conv_a_w[l], conv_a_b=row(conv_a_b[l]), ln_a_g=row(ln_a_g[l]), ln_a_b=row(ln_a_b[l]),
        lamp=jnp.stack([lam_q1[l], lam_k1[l], lam_q2[l], lam_k2[l]]).astype(F32),
        subln_w=row(subln_w[l]),
        conv_c_w=conv_c_w[l], conv_c_b=row(conv_c_b[l]),
        dtb=row(jnp.pad(dt_bias[l], (0, pad_h))), alog=row(jnp.pad(a_log[l], (0, pad_h))),
        dskip_e=row(jnp.repeat(d_skip[l], HEAD_DIM_C)), norm_c_w=row(norm_c_w[l]),
        w_mem_kv=w_mem_kv[l].astype(BF16),
        w_branch=w_branch[l].astype(BF16), w_out=w_out[l].astype(BF16),
        ln_g=row(ln_g[l]), ln_b=row(ln_b[l]),
        lam_init=0.8 - 0.6 * math.exp(-0.3 * l),
    )


def _layer(x32, x16, lw, project_kv, attend, mem, conv_a_state, conv_c_state, ssm_state, expand, alpha):
    bsz, t, d = x32.shape
    m = bsz * t
    xf = x16.reshape(m, d)
    wide = 2 * BR_WIDTH
    u = _proj(xf, lw["w_glu"], epilogue="glu", out_dtype=F32).reshape(bsz, t, BR_WIDTH)
    silu_all = _proj(xf, lw["w_silu"], epilogue="silu", out_dtype=BF16, tn_pref=wide).reshape(bsz, t, 4 * BR_WIDTH)
    qm = _proj(xf, lw["w_qm"], epilogue="none", out_dtype=BF16, tn_pref=wide).reshape(bsz, t, 2 * BR_WIDTH)
    kv32, kv16 = project_kv(xf)
    xdt = _proj(xf, lw["w_xdt"], epilogue="none", out_dtype=F32, tn_pref=CONV_C_DIM + LANES)
    gates = _proj(xf, lw["w_g"], epilogue="sigmoid_bias", out_dtype=BF16, bias=lw["b_gate"], tn_pref=wide)
    kv16 = kv16.reshape(bsz, t, -1)
    xdt = xdt.reshape(bsz, t, CONV_C_DIM + LANES)

    y_a, new_conv_a = _conv(u, conv_a_state, lw["conv_a_w"], lw["conv_a_b"], post="ln_swish_gate", out_dtype=BF16,
                            ln_g=lw["ln_a_g"], ln_b=lw["ln_a_b"], gate=silu_all, gate_col=0)
    y_b = attend(qm, kv32, kv16, silu_all, lw)
    xc, new_conv_c = _conv(xdt, conv_c_state, lw["conv_c_w"], lw["conv_c_b"], post="silu", out_dtype=F32)
    y_c, new_h = _ssd(xc, xdt, CONV_C_DIM // LANES, silu_all, 2, ssm_state, lw["dtb"], lw["alog"], expand,
                      lw["dskip_e"], lw["norm_c_w"])
    y_m = _mem_attn(qm, mem[0], mem[1], mem[2], mem[3], silu_all)
    flat = lambda a: a.reshape(m, -1)
    x32n, x16n = _merge(flat(y_a), flat(y_b), flat(y_c), flat(y_m), gates, lw["w_branch"], lw["w_out"],
                        x32.reshape(m, d), lw["ln_g"], lw["ln_b"], alpha)
    return x32n.reshape(bsz, t, d), x16n.reshape(bsz, t, d), kv32, new_conv_a, new_conv_c, new_h


def kernel(x_prompt, x_sample, mem_prompt, cache_k, cache_v, page_table, cache_mem_k, cache_mem_v, state_conv_a, state_conv_c, state_ssm, w_in, b_gate, conv_a_w, conv_a_b, ln_a_g, ln_a_b, lam_q1, lam_k1, lam_q2, lam_k2, subln_w, conv_c_w, conv_c_b, dt_bias, a_log, d_skip, norm_c_w, w_mem_kv, w_branch, w_out, ln_g, ln_b):
    bsz, seq, d = x_prompt.shape
    db, t_new, _ = x_sample.shape
    depth = w_in.shape[0]
    n_phys, _, page, _, _ = cache_k.shape
    n_mem = mem_prompt.shape[1]
    alpha = (2 * depth) ** 0.25
    kvw = KV_HEADS_B * V_DIM_B

    ck = cache_k.reshape(n_phys, depth, page * KV_HEADS_B, V_DIM_B)
    cv = cache_v.reshape(n_phys, depth, page * KV_HEADS_B, V_DIM_B)
    cmk = cache_mem_k.reshape(db, depth, n_mem, BR_WIDTH)
    cmv = cache_mem_v.reshape(db, depth, n_mem, BR_WIDTH)
    mem16 = mem_prompt.reshape(bsz * n_mem, d).astype(BF16)
    head_of_lane = jnp.arange(D_INNER_C, dtype=jnp.int32) // HEAD_DIM_C
    expand = (jnp.arange(LANES, dtype=jnp.int32)[:, None] == head_of_lane[None, :]).astype(BF16)
    blk_of_row = jnp.arange(2 * KV_HEADS_B * GROUP_B * t_new, dtype=jnp.int32) // (GROUP_B * t_new)
    blk_of_col = jnp.arange(kvw, dtype=jnp.int32) // HEAD_DIM_B
    qmask = (blk_of_row[:, None] == blk_of_col[None, :])

    xp32, xp16 = x_prompt, x_prompt.astype(BF16)
    xs32, xs16 = x_sample, x_sample.astype(BF16)
    outs = {k: [] for k in ("mkp", "mvp", "cap", "ccp", "hp", "ks", "vs", "cas", "ccs", "hs")}
    pages = (jnp.zeros((bsz, seq // page, depth, page * KV_HEADS_B, V_DIM_B), F32),) * 2
    for l in range(depth):
        lw = _layer_weights(l, w_in, b_gate, conv_a_w, conv_a_b, ln_a_g, ln_a_b, lam_q1, lam_k1, lam_q2, lam_k2,
                            subln_w, conv_c_w, conv_c_b, dt_bias, a_log, d_skip, norm_c_w, w_mem_kv, w_branch,
                            w_out, ln_g, ln_b)
        mkv32 = _proj(mem16, lw["w_mem_kv"], epilogue="none", out_dtype=F32)
        mkv_heads = mkv32.reshape(bsz, 1, n_mem, 2 * BR_WIDTH)

        def attend_prompt(qm, kv32, kv16, silu_all, lw):
            return _attn_prompt(qm, kv16, silu_all, lw["lamp"], lw["subln_w"], lw["lam_init"])

        def project_kv_prompt(xf, lw=lw, l=l, pages=pages):
            k_pages, v_pages, kv16 = _proj_kv_pages(xf, lw["w_kv"], pages[0], pages[1], l, seq)
            return (k_pages, v_pages), kv16

        xp32, xp16, pages, ca, cc, hh = _layer(xp32, xp16, lw, project_kv_prompt, attend_prompt,
                                               (mkv_heads, (0, 0), mkv_heads, (0, 1)),
                                               None, None, None, expand, alpha)
        mkv32 = mkv32.reshape(bsz, n_mem, 2 * BR_WIDTH)
        outs["mkp"].append(mkv32[..., :BR_WIDTH].reshape(bsz, n_mem, HEADS_M, HEAD_DIM_M))
        outs["mvp"].append(mkv32[..., BR_WIDTH:].reshape(bsz, n_mem, HEADS_M, HEAD_DIM_M))
        outs["cap"].append(ca); outs["ccp"].append(cc); outs["hp"].append(hh)

        def attend_sample(qm, kv32, kv16, silu_all, lw, l=l):
            q = qm[..., :BR_WIDTH].reshape(db, t_new, KV_HEADS_B, GROUP_B, 2, HEAD_DIM_B)
            q = q.transpose(0, 2, 4, 3, 1, 5).reshape(db, 2 * KV_HEADS_B * GROUP_B * t_new, 1, HEAD_DIM_B)
            qblk = jnp.where(qmask.reshape(1, -1, 2 * KV_HEADS_B, HEAD_DIM_B), q, jnp.zeros((), BF16))
            qblk = qblk.reshape(db, -1, kvw)
            return _attn_sample(qblk, ck, cv, page_table, l, kv32, silu_all, lw["lamp"], lw["subln_w"],
                                lw["lam_init"])

        def project_kv_sample(xf, lw=lw):
            kv32, kv16 = _proj(xf, lw["w_kv"], epilogue="dual")
            return kv32.reshape(db, t_new, 2 * kvw), kv16

        xs32, xs16, kv32, ca, cc, hh = _layer(xs32, xs16, lw, project_kv_sample, attend_sample,
                                               (cmk, (l, 0), cmv, (l, 0)),
                                               state_conv_a[:, l], state_conv_c[:, l], state_ssm[:, l],
                                               expand, alpha)
        outs["ks"].append(kv32[..., :kvw].reshape(db, t_new, KV_HEADS_B, V_DIM_B))
        outs["vs"].append(kv32[..., kvw:].reshape(db, t_new, KV_HEADS_B, V_DIM_B))
        outs["cas"].append(ca); outs["ccs"].append(cc); outs["hs"].append(hh)

    st = lambda k, axis: jnp.stack(outs[k], axis=axis)
    paged = lambda a: a.reshape(bsz, seq // page, depth, page, KV_HEADS_B, V_DIM_B)
    return (xp32, xs32, paged(pages[0]), paged(pages[1]), st("mkp", 1), st("mvp", 1), st("cap", 1), st("ccp", 1),
            st("hp", 1), st("ks", 1), st("vs", 1), st("cas", 1), st("ccs", 1), st("hs", 1))
```

```python
import functools
import math

import jax
import jax.numpy as jnp
from jax import lax
from jax.experimental import pallas as pl
from jax.experimental.pallas import tpu as pltpu

F32 = jnp.float32
BF16 = jnp.bfloat16

LANES = 128
SUBLANES = 8
VMEM_BYTES_V7X = 64 * 1024 * 1024
VMEM_LIMIT = (VMEM_BYTES_V7X * 3) // 4

BR_WIDTH = 1024
N_BRANCH = 4
CONV_A_WIDTH = 31
HEADS_B = 8
KV_HEADS_B = 4
GROUP_B = HEADS_B // KV_HEADS_B
HEAD_DIM_B = 64
V_DIM_B = 2 * HEAD_DIM_B
D_INNER_C = 1024
HEAD_DIM_C = 64
HEADS_C = D_INNER_C // HEAD_DIM_C
GROUPS_C = 4
D_STATE_C = 128
CONV_C_WIDTH = 4
CONV_C_DIM = D_INNER_C + 2 * GROUPS_C * D_STATE_C
SSD_CHUNK = 128
HEADS_M = 4
HEAD_DIM_M = BR_WIDTH // HEADS_M
EPS = 1e-5
NEG_INF = float("-inf")
LOG2E = math.log2(math.e)


def _tile(n, pref):
    if n <= pref:
        return n
    t = pref
    while n % t:
        t //= 2
    return t


def _sigmoid(x):
    return 1.0 / (1.0 + jnp.exp(-x))


def _silu(x):
    return x * _sigmoid(x)


def _params(sem):
    return pltpu.CompilerParams(dimension_semantics=sem, vmem_limit_bytes=VMEM_LIMIT)


def _proj_kernel(*refs, epilogue):
    x = refs[0][...]
    if epilogue == "glu":
        _, w1_ref, w2_ref, o_ref = refs
        a1 = jnp.dot(x, w1_ref[...], preferred_element_type=F32)
        a2 = jnp.dot(x, w2_ref[...], preferred_element_type=F32)
        o_ref[...] = (a1 * _sigmoid(a2)).astype(o_ref.dtype)
    elif epilogue == "sigmoid_bias":
        _, w_ref, b_ref, o_ref = refs
        y = jnp.dot(x, w_ref[...], preferred_element_type=F32)
        o_ref[...] = _sigmoid(y + b_ref[...]).astype(o_ref.dtype)
    elif epilogue == "dual":
        _, w_ref, o32_ref, o16_ref = refs
        y = jnp.dot(x, w_ref[...], preferred_element_type=F32)
        o32_ref[...] = y
        o16_ref[...] = y.astype(BF16)
    elif epilogue == "kv_pages":
        _, w_ref, _, _, ko_ref, vo_ref, o16_ref = refs
        y = jnp.dot(x, w_ref[...], preferred_element_type=F32)
        o16_ref[...] = y.astype(BF16)
        page = ko_ref.shape[3] // KV_HEADS_B
        kvw = KV_HEADS_B * V_DIM_B
        for j in range(ko_ref.shape[1]):
            for h in range(KV_HEADS_B):
                rows = pl.ds(h, page, stride=KV_HEADS_B)
                ko_ref[0, j, 0, rows, :] = y[j * page:(j + 1) * page, h * V_DIM_B:(h + 1) * V_DIM_B]
                vo_ref[0, j, 0, rows, :] = y[j * page:(j + 1) * page, kvw + h * V_DIM_B:kvw + (h + 1) * V_DIM_B]
    else:
        _, w_ref, o_ref = refs
        y = jnp.dot(x, w_ref[...], preferred_element_type=F32)
        if epilogue == "silu":
            y = _silu(y)
        o_ref[...] = y.astype(o_ref.dtype)


def _proj(x, w, *, epilogue, out_dtype=F32, bias=None, tn_pref=1024, tm_pref=1024):
    m, k = x.shape
    n = w.shape[1] // 2 if epilogue == "glu" else w.shape[1]
    tm, tn = _tile(m, tm_pref), _tile(n, tn_pref)
    nj = n // tn
    x_spec = pl.BlockSpec((tm, k), lambda j, i: (i, 0))
    w_spec = pl.BlockSpec((k, tn), lambda j, i: (0, j))
    o_spec = pl.BlockSpec((tm, tn), lambda j, i: (i, j))
    in_specs, args = [x_spec, w_spec], [x, w]
    out_shape, out_specs = jax.ShapeDtypeStruct((m, n), out_dtype), o_spec
    if epilogue == "glu":
        in_specs.append(pl.BlockSpec((k, tn), lambda j, i: (0, j + nj)))
        args.append(w)
    elif epilogue == "sigmoid_bias":
        in_specs.append(pl.BlockSpec((1, tn), lambda j, i: (0, j)))
        args.append(bias)
    elif epilogue == "dual":
        out_shape = (jax.ShapeDtypeStruct((m, n), F32), jax.ShapeDtypeStruct((m, n), BF16))
        out_specs = (o_spec, o_spec)
    return pl.pallas_call(
        functools.partial(_proj_kernel, epilogue=epilogue),
        grid=(nj, m // tm),
        in_specs=in_specs,
        out_specs=out_specs,
        out_shape=out_shape,
        compiler_params=_params(("parallel", "parallel")),
        name="proj_" + epilogue,
    )(*args)


def _proj_kv_pages(x, w, k_pages, v_pages, layer, seq, *, tm_pref=1024):
    m, k = x.shape
    n = w.shape[1]
    page = k_pages.shape[3] // KV_HEADS_B
    tm = _tile(seq, tm_pref)
    per_seq = seq // tm
    pg_spec = pl.BlockSpec((1, tm // page, 1, page * KV_HEADS_B, V_DIM_B),
                           lambda j, i: (i // per_seq, i % per_seq, layer, 0, 0))
    return pl.pallas_call(
        functools.partial(_proj_kernel, epilogue="kv_pages"),
        grid=(1, m // tm),
        in_specs=[pl.BlockSpec((tm, k), lambda j, i: (i, 0)), pl.BlockSpec((k, n), lambda j, i: (0, 0)),
                  pl.BlockSpec(memory_space=pl.ANY), pl.BlockSpec(memory_space=pl.ANY)],
        out_specs=(pg_spec, pg_spec, pl.BlockSpec((tm, n), lambda j, i: (i, 0))),
        out_shape=(jax.ShapeDtypeStruct(k_pages.shape, F32), jax.ShapeDtypeStruct(v_pages.shape, F32),
                   jax.ShapeDtypeStruct((m, n), BF16)),
        input_output_aliases={2: 0, 3: 1},
        compiler_params=_params(("parallel", "parallel")),
        name="proj_kv_pages",
    )(x, w, k_pages, v_pages)


def _conv_kernel(*refs, taps, hist, tt, rb, post, has_state):
    refs = list(refs)
    x_ref = refs.pop(0)
    state_ref = refs.pop(0) if has_state else None
    w_ref, b_ref = refs.pop(0), refs.pop(0)
    if post == "ln_swish_gate":
        g_ref, beta_ref, gate_ref = refs.pop(0), refs.pop(0), refs.pop(0)
    y_ref, ns_ref, win_ref, sh_ref = refs
    t = pl.program_id(1)
    back = taps - 1
    c = x_ref.shape[-1]
    n_win = hist + tt

    @pl.when(t == 0)
    def _():
        win_ref[0:hist, :] = jnp.zeros((hist, c), F32)
        win_ref[n_win:n_win + SUBLANES, :] = jnp.zeros((SUBLANES, c), F32)
        if has_state:
            win_ref[hist - back:hist, :] = state_ref[0].astype(F32)

    win_ref[hist:n_win, :] = x_ref[0].astype(F32)

    offsets = [j + hist - back for j in range(taps)]
    for p in sorted({o % SUBLANES for o in offsets} - {0}):
        sh_ref[p - 1, :, :] = win_ref[p:p + n_win, :]

    for r0 in range(0, tt, rb):
        acc = jnp.broadcast_to(b_ref[...], (rb, c))
        for j, o in enumerate(offsets):
            a, p = divmod(o, SUBLANES)
            lo = r0 + a * SUBLANES
            xs = win_ref[lo:lo + rb, :] if p == 0 else sh_ref[p - 1, lo:lo + rb, :]
            wj = w_ref[j * SUBLANES:(j + 1) * SUBLANES, :]
            acc = acc + (wj if rb == SUBLANES else jnp.concatenate([wj] * (rb // SUBLANES), axis=0)) * xs
        if post == "ln_swish_gate":
            mu = jnp.mean(acc, axis=-1, keepdims=True)
            d = acc - mu
            var = jnp.mean(d * d, axis=-1, keepdims=True)
            y = d * lax.rsqrt(var + EPS) * g_ref[...] + beta_ref[...]
            y = _silu(y) * gate_ref[0, r0:r0 + rb, :].astype(F32)
        else:
            y = _silu(acc)
        y_ref[0, r0:r0 + rb, :] = y.astype(y_ref.dtype)

    @pl.when(t == pl.num_programs(1) - 1)
    def _():
        ns_ref[0] = win_ref[n_win - back:n_win, :]

    win_ref[0:hist, :] = win_ref[tt:tt + hist, :]


def _conv(x, state, w, b, *, post, out_dtype, ln_g=None, ln_b=None, gate=None, gate_col=0, tt_pref=256):
    bsz, t, _ = x.shape
    taps, c = w.shape
    back = taps - 1
    hist = -(-back // SUBLANES) * SUBLANES
    tt = _tile(t, tt_pref)
    rb = min(tt, max(SUBLANES, (32 * 1024) // c))
    has_state = state is not None
    in_specs = [pl.BlockSpec((1, tt, c), lambda i, j: (i, j, 0))]
    args = [x]
    if has_state:
        in_specs.append(pl.BlockSpec((1, back, c), lambda i, j: (i, 0, 0)))
        args.append(state)
    in_specs += [pl.BlockSpec((taps * SUBLANES, c), lambda i, j: (0, 0)), pl.BlockSpec((1, c), lambda i, j: (0, 0))]
    args += [jnp.repeat(w, SUBLANES, axis=0), b]
    if post == "ln_swish_gate":
        in_specs += [pl.BlockSpec((1, c), lambda i, j: (0, 0)), pl.BlockSpec((1, c), lambda i, j: (0, 0)),
                     pl.BlockSpec((1, tt, c), lambda i, j: (i, j, gate_col))]
        args += [ln_g, ln_b, gate]
    return pl.pallas_call(
        functools.partial(_conv_kernel, taps=taps, hist=hist, tt=tt, rb=rb, post=post, has_state=has_state),
        grid=(bsz, t // tt),
        in_specs=in_specs,
        out_specs=(pl.BlockSpec((1, tt, c), lambda i, j: (i, j, 0)),
                   pl.BlockSpec((1, back, c), lambda i, j: (i, 0, 0))),
        out_shape=(jax.ShapeDtypeStruct((bsz, t, c), out_dtype), jax.ShapeDtypeStruct((bsz, back, c), F32)),
        scratch_shapes=[pltpu.VMEM((hist + tt + SUBLANES, c), F32),
                        pltpu.VMEM((SUBLANES - 1, hist + tt, c), F32)],
        compiler_params=_params(("parallel", "arbitrary")),
        name="conv_" + post,
    )(*args)


def _split3(x):
    hi = x.astype(BF16)
    r1 = x - hi.astype(F32)
    mid = r1.astype(BF16)
    lo = (r1 - mid.astype(F32)).astype(BF16)
    return hi, mid, lo


def _dot01(a01, x, *, left):
    out = None
    for piece in _split3(x):
        y = (jnp.dot(a01, piece, preferred_element_type=F32) if left
             else jnp.dot(piece, a01, preferred_element_type=F32))
        out = y if out is None else out + y
    return out


def _ssd_kernel(*refs, q, t_valid, has_h0):
    refs = list(refs)
    xc_ref, dt_ref, sz_ref = refs.pop(0), refs.pop(0), refs.pop(0)
    h0_ref = refs.pop(0) if has_h0 else None
    dtb_ref, alog_ref, e_ref, dskip_ref, nw_ref, y_ref, hout_ref, ht_ref = refs
    ci = pl.program_id(1)
    rows = xc_ref.shape[1]
    nstate = GROUPS_C * D_STATE_C

    @pl.when(ci == 0)
    def _():
        if has_h0:
            ht_ref[...] = h0_ref[0].reshape(D_INNER_C, D_STATE_C).T
        else:
            ht_ref[...] = jnp.zeros(ht_ref.shape, F32)

    xc = xc_ref[0]
    dtr = dt_ref[0]
    if rows < q:
        xc = jnp.concatenate([xc, jnp.zeros((q - rows, xc.shape[1]), F32)], axis=0)
        dtr = jnp.concatenate([dtr, jnp.zeros((q - rows, dtr.shape[1]), F32)], axis=0)
    x = xc[:, :D_INNER_C]
    bm = xc[:, D_INNER_C:D_INNER_C + nstate]
    cm = xc[:, D_INNER_C + nstate:]

    z = dtr + dtb_ref[...]
    dt = jnp.maximum(z, 0.0) + jnp.log1p(jnp.exp(-jnp.abs(z)))
    row_id = lax.broadcasted_iota(jnp.int32, (q, LANES), 0)
    if t_valid < q:
        dt = jnp.where(row_id < t_valid, dt, 0.0)
    a_neg = -jnp.exp(alog_ref[...])
    r_i = lax.broadcasted_iota(jnp.int32, (q, q), 0)
    c_i = lax.broadcasted_iota(jnp.int32, (q, q), 1)
    causal = r_i >= c_i
    tril = jnp.where(causal, 1.0, 0.0).astype(BF16)
    cum = _dot01(tril, dt * a_neg, left=True)
    cum_last = cum[q - 1:q, :]
    expand = e_ref[...]
    dt_e = _dot01(expand, dt, left=False)
    ecum_e = _dot01(expand, jnp.exp(cum), left=False)
    toend_e = _dot01(expand, jnp.exp(cum_last - cum), left=False)
    cum_t = cum.T
    xdt = x * dt_e
    xdt16 = xdt.astype(BF16)
    xdte16 = (xdt * toend_e).astype(BF16)
    lane = lax.broadcasted_iota(jnp.int32, (q, LANES), 1)

    hg = HEADS_C // GROUPS_C
    gw = hg * HEAD_DIM_C
    ys = []
    for g in range(GROUPS_C):
        bg = bm[:, g * D_STATE_C:(g + 1) * D_STATE_C]
        cg16 = cm[:, g * D_STATE_C:(g + 1) * D_STATE_C].astype(BF16)
        bgt16 = bg.T.astype(BF16)
        cb = jnp.dot(cg16, bgt16, preferred_element_type=F32)
        h_prev = ht_ref[:, g * gw:(g + 1) * gw]
        y_inter = jnp.dot(cg16, h_prev.astype(BF16), preferred_element_type=F32) * ecum_e[:, g * gw:(g + 1) * gw]
        pairs = []
        for k in range(hg // 2):
            xp = xdt16[:, g * gw + k * LANES:g * gw + (k + 1) * LANES]
            halves = []
            for hh in range(2):
                h = g * hg + 2 * k + hh
                seg = cum[:, h:h + 1] - cum_t[h:h + 1, :]
                dec = jnp.exp(jnp.where(causal, seg, NEG_INF))
                halves.append(jnp.dot((cb * dec).astype(BF16), xp, preferred_element_type=F32))
            pairs.append(jnp.where(lane < HEAD_DIM_C, halves[0], halves[1]))
        ys.append(jnp.concatenate(pairs, axis=1) + y_inter)
        ht_ref[:, g * gw:(g + 1) * gw] = (
            h_prev * ecum_e[q - 1:q, g * gw:(g + 1) * gw]
            + jnp.dot(bgt16, xdte16[:, g * gw:(g + 1) * gw], preferred_element_type=F32))
    y = jnp.concatenate(ys, axis=1) + dskip_ref[...] * x
    y = y * sz_ref[0].astype(F32) if rows == q else y[:rows] * sz_ref[0].astype(F32)
    y = y * lax.rsqrt(jnp.mean(y * y, axis=-1, keepdims=True) + EPS) * nw_ref[...]
    y_ref[0] = y.astype(y_ref.dtype)

    @pl.when(ci == pl.num_programs(1) - 1)
    def _():
        hout_ref[0] = ht_ref[...].T.reshape(HEADS_C, HEAD_DIM_C, D_STATE_C)


def _ssd(xc, dt_raw, dt_col, sz, sz_col, h0, dtb, alog, expand, dskip_e, normw):
    bsz, t, _ = xc.shape
    q = SSD_CHUNK
    rows = q if t % q == 0 else t
    assert rows <= q and rows % SUBLANES == 0
    nc = t // rows
    has_h0 = h0 is not None
    in_specs = [pl.BlockSpec((1, rows, CONV_C_DIM), lambda b, c: (b, c, 0)),
                pl.BlockSpec((1, rows, LANES), lambda b, c: (b, c, dt_col)),
                pl.BlockSpec((1, rows, D_INNER_C), lambda b, c: (b, c, sz_col))]
    args = [xc, dt_raw, sz]
    if has_h0:
        in_specs.append(pl.BlockSpec((1, HEADS_C, HEAD_DIM_C, D_STATE_C), lambda b, c: (b, 0, 0, 0)))
        args.append(h0)
    in_specs += [pl.BlockSpec((1, LANES), lambda b, c: (0, 0)), pl.BlockSpec((1, LANES), lambda b, c: (0, 0)),
                 pl.BlockSpec((LANES, D_INNER_C), lambda b, c: (0, 0)),
                 pl.BlockSpec((1, D_INNER_C), lambda b, c: (0, 0)), pl.BlockSpec((1, D_INNER_C), lambda b, c: (0, 0))]
    args += [dtb, alog, expand, dskip_e, normw]
    return pl.pallas_call(
        functools.partial(_ssd_kernel, q=q, t_valid=rows, has_h0=has_h0),
        grid=(bsz, nc),
        in_specs=in_specs,
        out_specs=(pl.BlockSpec((1, rows, D_INNER_C), lambda b, c: (b, c, 0)),
                   pl.BlockSpec((1, HEADS_C, HEAD_DIM_C, D_STATE_C), lambda b, c: (b, 0, 0, 0))),
        out_shape=(jax.ShapeDtypeStruct((bsz, t, D_INNER_C), BF16),
                   jax.ShapeDtypeStruct((bsz, HEADS_C, HEAD_DIM_C, D_STATE_C), F32)),
        scratch_shapes=[pltpu.VMEM((D_STATE_C, D_INNER_C), F32)],
        compiler_params=_params(("parallel", "arbitrary")),
        name="ssd",
    )(*args)


def _lam(lamp_ref, lam_init):
    p = lamp_ref[...]
    s1 = jnp.sum(p[0:1] * p[1:2], axis=-1, keepdims=True)
    s2 = jnp.sum(p[2:3] * p[3:4], axis=-1, keepdims=True)
    return jnp.exp(s1) - jnp.exp(s2) + lam_init


def _diff_finish(o0, o1, lam, subw, lam_init, gate):
    o = o0 - lam * o1
    o = o * lax.rsqrt(jnp.mean(o * o, axis=-1, keepdims=True) + EPS) * subw
    return o * (1.0 - lam_init) * gate


def _online_softmax_step(s, v16, m_ref, l_ref, acc_ref):
    m_prev = m_ref[...]
    m_new = jnp.maximum(m_prev, jnp.max(s, axis=-1, keepdims=True))
    p = jnp.exp2(s - m_new)
    alpha = jnp.exp2(m_prev - m_new)
    l_ref[...] = alpha * l_ref[...] + jnp.sum(p, axis=-1, keepdims=True)
    acc_ref[...] = alpha * acc_ref[...] + jnp.dot(p.astype(BF16), v16, preferred_element_type=F32)
    m_ref[...] = m_new


def _lane_tile(x, width):
    n = width // LANES
    return x if n == 1 else jnp.concatenate([x] * n, axis=1)


def _attn_prompt_kernel(q_ref, k_ref, v_ref, gate_ref, lamp_ref, subw_ref, o_ref,
                        qs_ref, v1_ref, m_ref, acc_ref, p_ref, al_ref, *, tq, tk, lam_init):
    qi = pl.program_id(2)
    nblk = 2 * GROUP_B
    nt = (((1,), (1,)), ((), ()))

    @pl.when(qi == 0)
    def _():
        v1_ref[:, :V_DIM_B] = v_ref[0]
        v1_ref[:, V_DIM_B:] = jnp.ones((v1_ref.shape[0], V_DIM_B), BF16)

    lane = lax.broadcasted_iota(jnp.int32, (tq, LANES), 1)
    for g in range(GROUP_B):
        qg = q_ref[0, :, g * LANES:(g + 1) * LANES]
        for c in range(2):
            keep = (lane >= HEAD_DIM_B) if c else (lane < HEAD_DIM_B)
            qs_ref[(2 * g + c) * tq:(2 * g + c + 1) * tq, :] = jnp.where(keep, qg, jnp.zeros_like(qg))
    m_ref[...] = jnp.full(m_ref.shape, NEG_INF, F32)
    acc_ref[...] = jnp.zeros(acc_ref.shape, F32)

    def step(start, width, diagonal):
        kt = k_ref[0, pl.ds(start, width), :]
        vt = v1_ref[pl.ds(start, width), :]
        if diagonal:
            keep = (lax.broadcasted_iota(jnp.int32, (tq, width), 0)
                    >= lax.broadcasted_iota(jnp.int32, (tq, width), 1))
        for i in range(nblk):
            rows = slice(i * tq, (i + 1) * tq)
            s = lax.dot_general(qs_ref[rows, :], kt, nt, preferred_element_type=F32)
            if diagonal:
                s = jnp.where(keep, s, NEG_INF)
            m_prev = m_ref[rows, :]
            m_new = jnp.maximum(m_prev, jnp.max(s, axis=-1, keepdims=True))
            p = jnp.exp2(s - _lane_tile(m_new, width))
            alpha = jnp.exp2(m_prev - m_new)
            pv = jnp.dot(p.astype(BF16), vt, preferred_element_type=F32)
            acc_ref[rows, :] = _lane_tile(alpha, 2 * V_DIM_B) * acc_ref[rows, :] + pv
            m_ref[rows, :] = m_new

    q0 = qi * tq
    n_wide = q0 // tk

    def wide_scores(rows, kt):
        s = lax.dot_general(qs_ref[rows, :], kt, nt, preferred_element_type=F32)
        m_prev = m_ref[rows, :]
        m_new = jnp.maximum(m_prev, jnp.max(s, axis=-1, keepdims=True))
        m_ref[rows, :] = m_new
        p_ref[rows, :] = jnp.exp2(s - _lane_tile(m_new, tk)).astype(BF16)
        al_ref[rows, :] = jnp.exp2(m_prev - m_new)

    def wide_values(rows, vt):
        pv = jnp.dot(p_ref[rows, :], vt, preferred_element_type=F32)
        acc_ref[rows, :] = _lane_tile(al_ref[rows, :], 2 * V_DIM_B) * acc_ref[rows, :] + pv

    @pl.when(n_wide > 0)
    def _():
        kt = k_ref[0, pl.ds(0, tk), :]
        for i in range(nblk):
            wide_scores(slice(i * tq, (i + 1) * tq), kt)

    def wide_body(ki, carry):
        kt = k_ref[0, pl.ds(pl.multiple_of(ki * tk, tk), tk), :]
        vt_prev = v1_ref[pl.ds(pl.multiple_of((ki - 1) * tk, tk), tk), :]
        for i in range(nblk):
            rows = slice(i * tq, (i + 1) * tq)
            wide_values(rows, vt_prev)
            wide_scores(rows, kt)
        return carry

    lax.fori_loop(1, n_wide, wide_body, 0)

    @pl.when(n_wide > 0)
    def _():
        vt_last = v1_ref[pl.ds(pl.multiple_of((n_wide - 1) * tk, tk), tk), :]
        for i in range(nblk):
            wide_values(slice(i * tq, (i + 1) * tq), vt_last)

    def narrow_body(j, carry):
        step(pl.multiple_of(n_wide * tk + j * tq, tq), tq, False)
        return carry

    if tk > tq:
        lax.fori_loop(0, (q0 - n_wide * tk) // tq, narrow_body, 0)
    step(pl.multiple_of(q0, tq), tq, True)

    lam = _lam(lamp_ref, lam_init)
    for g in range(GROUP_B):
        r0, r1 = (2 * g) * tq, (2 * g + 1) * tq
        o0 = acc_ref[r0:r0 + tq, :V_DIM_B] / acc_ref[r0:r0 + tq, V_DIM_B:]
        o1 = acc_ref[r1:r1 + tq, :V_DIM_B] / acc_ref[r1:r1 + tq, V_DIM_B:]
        gate = gate_ref[0, :, g * LANES:(g + 1) * LANES].astype(F32)
        o_ref[0, :, g * LANES:(g + 1) * LANES] = _diff_finish(
            o0, o1, lam, subw_ref[...], lam_init, gate).astype(o_ref.dtype)


def _attn_prompt(qm, kv16, silu_all, lamp, subw, lam_init, *, tq_pref=512, tk_pref=512):
    bsz, s, _ = kv16.shape
    tq = _tile(s, tq_pref)
    tk = _tile(s, tk_pref)
    assert tk % tq == 0
    hw = GROUP_B * V_DIM_B
    gate_blk0 = BR_WIDTH // hw
    nrow = 2 * GROUP_B * tq
    return pl.pallas_call(
        functools.partial(_attn_prompt_kernel, tq=tq, tk=tk, lam_init=lam_init),
        grid=(bsz, KV_HEADS_B, s // tq),
        in_specs=[pl.BlockSpec((1, tq, hw), lambda b, h, i: (b, i, h)),
                  pl.BlockSpec((1, s, V_DIM_B), lambda b, h, i: (b, 0, h)),
                  pl.BlockSpec((1, s, V_DIM_B), lambda b, h, i: (b, 0, KV_HEADS_B + h)),
                  pl.BlockSpec((1, tq, hw), lambda b, h, i: (b, i, gate_blk0 + h)),
                  pl.BlockSpec((4, HEAD_DIM_B), lambda b, h, i: (0, 0)),
                  pl.BlockSpec((1, V_DIM_B), lambda b, h, i: (0, 0))],
        out_specs=pl.BlockSpec((1, tq, hw), lambda b, h, i: (b, i, h)),
        out_shape=jax.ShapeDtypeStruct((bsz, s, BR_WIDTH), BF16),
        scratch_shapes=[pltpu.VMEM((nrow, LANES), BF16),
                        pltpu.VMEM((s, 2 * V_DIM_B), BF16),
                        pltpu.VMEM((nrow, LANES), F32),
                        pltpu.VMEM((nrow, 2 * V_DIM_B), F32),
                        pltpu.VMEM((nrow, tk), BF16),
                        pltpu.VMEM((nrow, LANES), F32)],
        compiler_params=_params(("parallel", "parallel", "arbitrary")),
        name="attn_prompt",
    )(qm, kv16, kv16, silu_all, lamp, subw)


def _attn_sample_kernel(*refs, pps, t_new, lam_init):
    pt_ref = refs[0]
    del pt_ref
    qb_ref = refs[1]
    k_refs = refs[2:2 + pps]
    v_refs = refs[2 + pps:2 + 2 * pps]
    kn_ref, vn_ref, gate_ref, lamp_ref, subw_ref, o_ref, m_ref, l_ref, acc_ref = refs[2 + 2 * pps:]
    si = pl.program_id(1)
    nrow = qb_ref.shape[1]
    kvw = KV_HEADS_B * V_DIM_B

    @pl.when(si == 0)
    def _():
        m_ref[...] = jnp.full(m_ref.shape, NEG_INF, F32)
        l_ref[...] = jnp.zeros(l_ref.shape, F32)
        acc_ref[...] = jnp.zeros(acc_ref.shape, F32)

    qb = qb_ref[0]
    nt = (((1,), (1,)), ((), ()))
    page = k_refs[0].shape[2] // KV_HEADS_B

    def pages16(page_refs):
        return jnp.concatenate(
            [jnp.concatenate([r[0, 0, pl.ds(h, page, stride=KV_HEADS_B), :] for h in range(KV_HEADS_B)],
                             axis=1).astype(BF16) for r in page_refs],
            axis=0)

    k16 = pages16(k_refs)
    v16 = pages16(v_refs)
    s = lax.dot_general(qb, k16, nt, preferred_element_type=F32)
    _online_softmax_step(s, v16, m_ref, l_ref, acc_ref)

    @pl.when(si == pl.num_programs(1) - 1)
    def _():
        pad = jnp.zeros((LANES - t_new, kvw), BF16)
        kn = jnp.concatenate([kn_ref[0].astype(BF16), pad], axis=0)
        vn = jnp.concatenate([vn_ref[0].astype(BF16), pad], axis=0)
        sn = lax.dot_general(qb, kn, nt, preferred_element_type=F32)
        tok = lax.broadcasted_iota(jnp.int32, (nrow, LANES), 0) % t_new
        col = lax.broadcasted_iota(jnp.int32, (nrow, LANES), 1)
        sn = jnp.where(col <= tok, sn, NEG_INF)
        _online_softmax_step(sn, vn, m_ref, l_ref, acc_ref)
        lam = _lam(lamp_ref, lam_init)
        for h in range(KV_HEADS_B):
            for g in range(GROUP_B):
                r0 = ((h * 2 + 0) * GROUP_B + g) * t_new
                r1 = ((h * 2 + 1) * GROUP_B + g) * t_new
                cs = slice(h * V_DIM_B, (h + 1) * V_DIM_B)
                o0 = acc_ref[r0:r0 + t_new, cs] / l_ref[r0:r0 + t_new, :]
                o1 = acc_ref[r1:r1 + t_new, cs] / l_ref[r1:r1 + t_new, :]
                oc = slice((h * GROUP_B + g) * V_DIM_B, (h * GROUP_B + g + 1) * V_DIM_B)
                gate = gate_ref[0, :, oc].astype(F32)
                o_ref[0, :, oc] = _diff_finish(o0, o1, lam, subw_ref[...], lam_init, gate).astype(o_ref.dtype)


def _attn_sample(qblk, cache_k, cache_v, page_table, layer, kv32, silu_all, lamp, subw, lam_init):
    db, nrow, kvw = qblk.shape
    t_new = kv32.shape[1]
    page_rows = cache_k.shape[2]
    n_pages = page_table.shape[1]
    pps = _tile(n_pages, 16)
    steps = n_pages // pps

    def page_spec(i):
        return pl.BlockSpec((1, 1, page_rows, V_DIM_B), lambda b, s, pt: (pt[b, s * pps + i], layer, 0, 0))

    in_specs = [pl.BlockSpec((1, nrow, kvw), lambda b, s, pt: (b, 0, 0))]
    in_specs += [page_spec(i) for i in range(pps)] + [page_spec(i) for i in range(pps)]
    in_specs += [pl.BlockSpec((1, t_new, kvw), lambda b, s, pt: (b, 0, 0)),
                 pl.BlockSpec((1, t_new, kvw), lambda b, s, pt: (b, 0, 1)),
                 pl.BlockSpec((1, t_new, BR_WIDTH), lambda b, s, pt: (b, 0, 1)),
                 pl.BlockSpec((4, HEAD_DIM_B), lambda b, s, pt: (0, 0)),
                 pl.BlockSpec((1, V_DIM_B), lambda b, s, pt: (0, 0))]
    grid_spec = pltpu.PrefetchScalarGridSpec(
        num_scalar_prefetch=1,
        grid=(db, steps),
        in_specs=in_specs,
        out_specs=pl.BlockSpec((1, t_new, BR_WIDTH), lambda b, s, pt: (b, 0, 0)),
        scratch_shapes=[pltpu.VMEM((nrow, 1), F32), pltpu.VMEM((nrow, 1), F32), pltpu.VMEM((nrow, kvw), F32)])
    return pl.pallas_call(
        functools.partial(_attn_sample_kernel, pps=pps, t_new=t_new, lam_init=lam_init),
        grid_spec=grid_spec,
        out_shape=jax.ShapeDtypeStruct((db, t_new, BR_WIDTH), BF16),
        compiler_params=_params(("parallel", "arbitrary")),
        name="attn_sample",
    )(page_table, qblk, *([cache_k] * pps), *([cache_v] * pps), kv32, kv32, silu_all, lamp, subw)


def _mem_attn_kernel(q_ref, mk_ref, mv_ref, gate_ref, o_ref):
    nt = (((1,), (1,)), ((), ()))
    for h in range(HEADS_M):
        cs = slice(h * HEAD_DIM_M, (h + 1) * HEAD_DIM_M)
        s = lax.dot_general(q_ref[0, :, cs], mk_ref[0, 0, :, cs].astype(BF16), nt, preferred_element_type=F32)
        p = jnp.exp(s - jnp.max(s, axis=-1, keepdims=True))
        p = p / jnp.sum(p, axis=-1, keepdims=True)
        o = jnp.dot(p.astype(BF16), mv_ref[0, 0, :, cs].astype(BF16), preferred_element_type=F32)
        o_ref[0, :, cs] = (o * gate_ref[0, :, cs].astype(F32)).astype(o_ref.dtype)


def _mem_attn(qm, mk, mk_idx, mv, mv_idx, silu_all, *, tq_pref=512):
    bsz, t, _ = qm.shape
    n_mem = mk.shape[2]
    tq = _tile(t, tq_pref)
    return pl.pallas_call(
        _mem_attn_kernel,
        grid=(bsz, t // tq),
        in_specs=[pl.BlockSpec((1, tq, BR_WIDTH), lambda b, i: (b, i, 1)),
                  pl.BlockSpec((1, 1, n_mem, BR_WIDTH), lambda b, i: (b, mk_idx[0], 0, mk_idx[1])),
                  pl.BlockSpec((1, 1, n_mem, BR_WIDTH), lambda b, i: (b, mv_idx[0], 0, mv_idx[1])),
                  pl.BlockSpec((1, tq, BR_WIDTH), lambda b, i: (b, i, 3))],
        out_specs=pl.BlockSpec((1, tq, BR_WIDTH), lambda b, i: (b, i, 0)),
        out_shape=jax.ShapeDtypeStruct((bsz, t, BR_WIDTH), BF16),
        compiler_params=_params(("parallel", "parallel")),
        name="mem_attn",
    )(qm, mk, mv, silu_all)


def _merge_kernel(ya_ref, yb_ref, yc_ref, ym_ref, gates_ref, wb_ref, wo_ref, x_ref, g_ref, b_ref,
                  o32_ref, o16_ref, *, alpha):
    d = x_ref.shape[-1]
    mix = None
    for k, y_ref in enumerate((ya_ref, yb_ref, yc_ref, ym_ref)):
        pk = jnp.dot(y_ref[...], wb_ref[k], preferred_element_type=F32)
        pk = pk * gates_ref[:, k * d:(k + 1) * d].astype(F32)
        mix = pk if mix is None else mix + pk
    out = jnp.dot(mix.astype(BF16), wo_ref[...], preferred_element_type=F32)
    z = alpha * x_ref[...] + out
    mu = jnp.mean(z, axis=-1, keepdims=True)
    zc = z - mu
    var = jnp.mean(zc * zc, axis=-1, keepdims=True)
    y = zc * lax.rsqrt(var + EPS) * g_ref[...] + b_ref[...]
    o32_ref[...] = y
    o16_ref[...] = y.astype(BF16)


def _merge(ya, yb, yc, ym, gates, wb, wo, x, ln_g, ln_b, alpha, *, tm_pref=256):
    m, d = x.shape
    tm = _tile(m, tm_pref)
    row = lambda w: pl.BlockSpec((tm, w), lambda i: (i, 0))
    const = lambda shape: pl.BlockSpec(shape, lambda i: (0,) * len(shape))
    return pl.pallas_call(
        functools.partial(_merge_kernel, alpha=alpha),
        grid=(m // tm,),
        in_specs=[row(BR_WIDTH), row(BR_WIDTH), row(BR_WIDTH), row(BR_WIDTH), row(N_BRANCH * d),
                  const((N_BRANCH, BR_WIDTH, d)), const((d, d)), row(d), const((1, d)), const((1, d))],
        out_specs=(row(d), row(d)),
        out_shape=(jax.ShapeDtypeStruct((m, d), F32), jax.ShapeDtypeStruct((m, d), BF16)),
        compiler_params=_params(("parallel",)),
        name="merge",
    )(ya, yb, yc, ym, gates, wb, wo, x, ln_g, ln_b)


def _layer_weights(l, w_in, b_gate, conv_a_w, conv_a_b, ln_a_g, ln_a_b, lam_q1, lam_k1, lam_q2, lam_k2, subln_w,
                   conv_c_w, conv_c_b, dt_bias, a_log, d_skip, norm_c_w, w_mem_kv, w_branch, w_out, ln_g, ln_b):
    w = w_in[l]
    off, cols = 0, {}
    for name, n in (("a_glu", 2 * BR_WIDTH), ("a_gate", BR_WIDTH), ("b_q", BR_WIDTH), ("b_k", KV_HEADS_B * V_DIM_B),
                    ("b_v", KV_HEADS_B * V_DIM_B), ("b_gate", BR_WIDTH), ("c_z", D_INNER_C), ("c_xbc", CONV_C_DIM),
                    ("c_dt", HEADS_C), ("m_q", BR_WIDTH), ("m_gate", BR_WIDTH), ("g", N_BRANCH * w.shape[0])):
        cols[name] = w[:, off:off + n]
        off += n
    d = w.shape[0]
    pad_h = LANES - HEADS_C
    row = lambda v: v.reshape(1, -1).astype(F32)
    return dict(
        w_glu=cols["a_glu"].astype(BF16),
        w_silu=jnp.concatenate([cols["a_gate"], cols["b_gate"], cols["c_z"], cols["m_gate"]], axis=1).astype(BF16),
        w_qm=jnp.concatenate([cols["b_q"] * (HEAD_DIM_B ** -0.5 * LOG2E), cols["m_q"] * HEAD_DIM_M ** -0.5],
                             axis=1).astype(BF16),
        w_kv=jnp.concatenate([cols["b_k"], cols["b_v"]], axis=1).astype(BF16),
        w_xdt=jnp.concatenate([cols["c_xbc"], cols["c_dt"], jnp.zeros((d, pad_h), F32)], axis=1).astype(BF16),
        w_g=cols["g"].astype(BF16),
        b_gate=row(b_gate[l]),
        conv_a_w=conv_a_w[l], conv_a_b=row(conv_a_b[l]), ln_a_g=row(ln_a_g[l]), ln_a_b=row(ln_a_b[l]),
        lamp=jnp.stack([lam_q1[l], lam_k1[l], lam_q2[l], lam_k2[l]]).astype(F32),
        subln_w=row(subln_w[l]),
        conv_c_w=conv_c_w[l], conv_c_b=row(conv_c_b[l]),
        dtb=row(jnp.pad(dt_bias[l], (0, pad_h))), alog=row(jnp.pad(a_log[l], (0, pad_h))),
        dskip_e=row(jnp.repeat(d_skip[l], HEAD_DIM_C)), norm_c_w=row(norm_c_w[l]),
        w_mem_kv=w_mem_kv[l].astype(BF16),
        w_branch=w_branch[l].astype(BF16), w_out=w_out[l].astype(BF16),
        ln_g=row(ln_g[l]), ln_b=row(ln_b[l]),
        lam_init=0.8 - 0.6 * math.exp(-0.3 * l),
    )


def _layer(x32, x16, lw, project_kv, attend, mem, conv_a_state, conv_c_state, ssm_state, expand, alpha):
    bsz, t, d = x32.shape
    m = bsz * t
    xf = x16.reshape(m, d)
    wide = 2 * BR_WIDTH
    u = _proj(xf, lw["w_glu"], epilogue="glu", out_dtype=F32).reshape(bsz, t, BR_WIDTH)
    silu_all = _proj(xf, lw["w_silu"], epilogue="silu", out_dtype=BF16, tn_pref=wide).reshape(bsz, t, 4 * BR_WIDTH)
    qm = _proj(xf, lw["w_qm"], epilogue="none", out_dtype=BF16, tn_pref=wide).reshape(bsz, t, 2 * BR_WIDTH)
    kv32, kv16 = project_kv(xf)
    xdt = _proj(xf, lw["w_xdt"], epilogue="none", out_dtype=F32, tn_pref=CONV_C_DIM + LANES)
    gates = _proj(xf, lw["w_g"], epilogue="sigmoid_bias", out_dtype=BF16, bias=lw["b_gate"], tn_pref=wide)
    kv16 = kv16.reshape(bsz, t, -1)
    xdt = xdt.reshape(bsz, t, CONV_C_DIM + LANES)

    y_a, new_conv_a = _conv(u, conv_a_state, lw["conv_a_w"], lw["conv_a_b"], post="ln_swish_gate", out_dtype=BF16,
                            ln_g=lw["ln_a_g"], ln_b=lw["ln_a_b"], gate=silu_all, gate_col=0)
    y_b = attend(qm, kv32, kv16, silu_all, lw)
    xc, new_conv_c = _conv(xdt, conv_c_state, lw["conv_c_w"], lw["conv_c_b"], post="silu", out_dtype=F32)
    y_c, new_h = _ssd(xc, xdt, CONV_C_DIM // LANES, silu_all, 2, ssm_state, lw["dtb"], lw["alog"], expand,
                      lw["dskip_e"], lw["norm_c_w"])
    y_m = _mem_attn(qm, mem[0], mem[1], mem[2], mem[3], silu_all)
    flat = lambda a: a.reshape(m, -1)
    x32n, x16n = _merge(flat(y_a), flat(y_b), flat(y_c), flat(y_m), gates, lw["w_branch"], lw["w_out"],
                        x32.reshape(m, d), lw["ln_g"], lw["ln_b"], alpha)
    return x32n.reshape(bsz, t, d), x16n.reshape(bsz, t, d), kv32, new_conv_a, new_conv_c, new_h


def kernel(x_prompt, x_sample, mem_prompt, cache_k, cache_v, page_table, cache_mem_k, cache_mem_v, state_conv_a, state_conv_c, state_ssm, w_in, b_gate, conv_a_w, conv_a_b, ln_a_g, ln_a_b, lam_q1, lam_k1, lam_q2, lam_k2, subln_w, conv_c_w, conv_c_b, dt_bias, a_log, d_skip, norm_c_w, w_mem_kv, w_branch, w_out, ln_g, ln_b):
    bsz, seq, d = x_prompt.shape
    db, t_new, _ = x_sample.shape
    depth = w_in.shape[0]
    n_phys, _, page, _, _ = cache_k.shape
    n_mem = mem_prompt.shape[1]
    alpha = (2 * depth) ** 0.25
    kvw = KV_HEADS_B * V_DIM_B

    ck = cache_k.reshape(n_phys, depth, page * KV_HEADS_B, V_DIM_B)
    cv = cache_v.reshape(n_phys, depth, page * KV_HEADS_B, V_DIM_B)
    cmk = cache_mem_k.reshape(db, depth, n_mem, BR_WIDTH)
    cmv = cache_mem_v.reshape(db, depth, n_mem, BR_WIDTH)
    mem16 = mem_prompt.reshape(bsz * n_mem, d).astype(BF16)
    head_of_lane = jnp.arange(D_INNER_C, dtype=jnp.int32) // HEAD_DIM_C
    expand = (jnp.arange(LANES, dtype=jnp.int32)[:, None] == head_of_lane[None, :]).astype(BF16)
    blk_of_row = jnp.arange(2 * KV_HEADS_B * GROUP_B * t_new, dtype=jnp.int32) // (GROUP_B * t_new)
    blk_of_col = jnp.arange(kvw, dtype=jnp.int32) // HEAD_DIM_B
    qmask = (blk_of_row[:, None] == blk_of_col[None, :])

    xp32, xp16 = x_prompt, x_prompt.astype(BF16)
    xs32, xs16 = x_sample, x_sample.astype(BF16)
    outs = {k: [] for k in ("mkp", "mvp", "cap", "ccp", "hp", "ks", "vs", "cas", "ccs", "hs")}
    pages = (jnp.zeros((bsz, seq // page, depth, page * KV_HEADS_B, V_DIM_B), F32),) * 2
    for l in range(depth):
        lw = _layer_weights(l, w_in, b_gate, conv_a_w, conv_a_b, ln_a_g, ln_a_b, lam_q1, lam_k1, lam_q2, lam_k2,
                            subln_w, conv_c_w, conv_c_b, dt_bias, a_log, d_skip, norm_c_w, w_mem_kv, w_branch,
                            w_out, ln_g, ln_b)
        mkv32 = _proj(mem16, lw["w_mem_kv"], epilogue="none", out_dtype=F32)
        mkv_heads = mkv32.reshape(bsz, 1, n_mem, 2 * BR_WIDTH)

        def attend_prompt(qm, kv32, kv16, silu_all, lw):
            return _attn_prompt(qm, kv16, silu_all, lw["lamp"], lw["subln_w"], lw["lam_init"])

        def project_kv_prompt(xf, lw=lw, l=l, pages=pages):
            k_pages, v_pages, kv16 = _proj_kv_pages(xf, lw["w_kv"], pages[0], pages[1], l, seq)
            return (k_pages, v_pages), kv16

        xp32, xp16, pages, ca, cc, hh = _layer(xp32, xp16, lw, project_kv_prompt, attend_prompt,
                                               (mkv_heads, (0, 0), mkv_heads, (0, 1)),
                                               None, None, None, expand, alpha)
        mkv32 = mkv32.reshape(bsz, n_mem, 2 * BR_WIDTH)
        outs["mkp"].append(mkv32[..., :BR_WIDTH].reshape(bsz, n_mem, HEADS_M, HEAD_DIM_M))
        outs["mvp"].append(mkv32[..., BR_WIDTH:].reshape(bsz, n_mem, HEADS_M, HEAD_DIM_M))
        outs["cap"].append(ca); outs["ccp"].append(cc); outs["hp"].append(hh)

        def attend_sample(qm, kv32, kv16, silu_all, lw, l=l):
            q = qm[..., :BR_WIDTH].reshape(db, t_new, KV_HEADS_B, GROUP_B, 2, HEAD_DIM_B)
            q = q.transpose(0, 2, 4, 3, 1, 5).reshape(db, 2 * KV_HEADS_B * GROUP_B * t_new, 1, HEAD_DIM_B)
            qblk = jnp.where(qmask.reshape(1, -1, 2 * KV_HEADS_B, HEAD_DIM_B), q, jnp.zeros((), BF16))
            qblk = qblk.reshape(db, -1, kvw)
            return _attn_sample(qblk, ck, cv, page_table, l, kv32, silu_all, lw["lamp"], lw["subln_w"],
                                lw["lam_init"])

        def project_kv_sample(xf, lw=lw):
            kv32, kv16 = _proj(xf, lw["w_kv"], epilogue="dual")
            return kv32.reshape(db, t_new, 2 * kvw), kv16

        xs32, xs16, kv32, ca, cc, hh = _layer(xs32, xs16, lw, project_kv_sample, attend_sample,
                                               (cmk, (l, 0), cmv, (l, 0)),
                                               state_conv_a[:, l], state_conv_c[:, l], state_ssm[:, l],
                                               expand, alpha)
        outs["ks"].append(kv32[..., :kvw].reshape(db, t_new, KV_HEADS_B, V_DIM_B))
        outs["vs"].append(kv32[..., kvw:].reshape(db, t_new, KV_HEADS_B, V_DIM_B))
        outs["cas"].append(ca); outs["ccs"].append(cc); outs["hs"].append(hh)

    st = lambda k, axis: jnp.stack(outs[k], axis=axis)
    paged = lambda a: a.reshape(bsz, seq // page, depth, page, KV_HEADS_B, V_DIM_B)
    return (xp32, xs32, paged(pages[0]), paged(pages[1]), st("mkp", 1), st("mvp", 1), st("cap", 1), st("ccp", 1),
            st("hp", 1), st("ks", 1), st("vs", 1), st("cas", 1), st("ccs", 1), st("hs", 1))
```
